```python
import jax, jax.numpy as jnp
from jax import lax
import numpy as np

D_MODEL = 1024
BATCH = 8
SEQ = 4096
DEPTH = 2

HEAD_DIM = 64
ATTN_WIDTH = D_MODEL // 2
ATTN_HEADS = ATTN_WIDTH // HEAD_DIM
POOL_WIDTH = D_MODEL - ATTN_WIDTH
POOL_WINDOWS = (2, 4, 8, 16)
POOL_GROUPS = len(POOL_WINDOWS)
POOL_GROUP_DIM = POOL_WIDTH // POOL_GROUPS
MIX_WIDTH = ATTN_WIDTH + POOL_WIDTH
IN_WIDTH = 3 * ATTN_WIDTH + POOL_WIDTH
MOBA_BLOCK = 256
MOBA_TOP_K = 3
Q_CHUNK = 32
ROPE_THETA = 500000.0
ROT_DIM = HEAD_DIM // 4
D_FF = ((8 * D_MODEL // 3 + 255) // 256) * 256
CONV_WIDTH = 3
NORM_EPS = 1e-6
NEG_INF = -1e30

kernel_name = "hybrid_moba_pool_convffn_adaln"


def rms_norm(x, gain):
    xf = x.astype(jnp.float32)
    y = xf * lax.rsqrt(jnp.mean(xf * xf, axis=-1, keepdims=True) + NORM_EPS)
    return (y * gain.astype(jnp.float32)).astype(x.dtype)


def partial_rotary(x, cos, sin):
    half = ROT_DIM // 2
    x1 = x[..., :half]
    x2 = x[..., half:ROT_DIM]
    return jnp.concatenate([x1 * cos - x2 * sin, x2 * cos + x1 * sin, x[..., ROT_DIM:]], axis=-1)


def moba_attention(q, k, v):
    b, h, s, d = q.shape
    nb = -(-s // MOBA_BLOCK)
    pad = nb * MOBA_BLOCK - s
    kb = jnp.pad(k, ((0, 0), (0, 0), (0, pad), (0, 0))).reshape(b, h, nb, MOBA_BLOCK, d)
    vb = jnp.pad(v, ((0, 0), (0, 0), (0, pad), (0, 0))).reshape(b, h, nb, MOBA_BLOCK, d)
    kmean = jnp.mean(kb.astype(jnp.float32), axis=3)
    pos = jnp.arange(s)
    qblk = pos // MOBA_BLOCK
    gate = jnp.einsum('bhsd,bhnd->bhsn', q.astype(jnp.float32), kmean)
    past = jnp.arange(nb)[None, :] < qblk[:, None]
    gate = jnp.where(past, gate, NEG_INF)
    k_sel = min(MOBA_TOP_K, nb)
    _, top_idx = lax.top_k(gate, k_sel)
    top_valid = top_idx < qblk[:, None]
    own = jnp.broadcast_to(qblk[None, None, :, None], (b, h, s, 1))
    sel = jnp.concatenate([top_idx.astype(jnp.int32), own.astype(jnp.int32)], axis=-1)
    slot_valid = jnp.concatenate([top_valid, jnp.ones((b, h, s, 1), dtype=bool)], axis=-1)
    nc = s // Q_CHUNK

    def to_chunks(a):
        return jnp.moveaxis(a.reshape(b, h, nc, Q_CHUNK, *a.shape[3:]), 2, 0)

    bi = jnp.arange(b)[:, None, None, None]
    hi = jnp.arange(h)[None, :, None, None]
    offs = jnp.arange(MOBA_BLOCK)
    scale = d ** -0.5

    def chunk_fn(args):
        qc, selc, validc, tc = args
        kg = kb[bi, hi, selc]
        vg = vb[bi, hi, selc]
        sc = jnp.einsum('bhcd,bhcnkd->bhcnk', qc, kg).astype(jnp.float32) * scale
        kpos = selc[..., None] * MOBA_BLOCK + offs
        mask = validc[..., None] & (kpos <= tc[None, None, :, None, None])
        sc = jnp.where(mask, sc, NEG_INF)
        p = jax.nn.softmax(sc.reshape(b, h, Q_CHUNK, -1), axis=-1).reshape(sc.shape)
        return jnp.einsum('bhcnk,bhcnkd->bhcd', p.astype(vg.dtype), vg)

    out = lax.map(chunk_fn, (to_chunks(q), to_chunks(sel), to_chunks(slot_valid),
                             pos.reshape(nc, Q_CHUNK)))
    return jnp.moveaxis(out, 0, 2).reshape(b, h, s, d)


def multiscale_pool(p, w_pool, pool_scale):
    b, s, _ = p.shape
    pf = p.astype(jnp.float32)
    cs = jnp.pad(jnp.cumsum(pf, axis=1), ((0, 0), (1, 0), (0, 0)))
    t = jnp.arange(s)
    outs = []
    for g, w in enumerate(POOL_WINDOWS):
        sl = slice(g * POOL_GROUP_DIM, (g + 1) * POOL_GROUP_DIM)
        cg = cs[:, :, sl]
        lower = jnp.pad(cg[:, :s + 1 - w], ((0, 0), (w - 1, 0), (0, 0)))
        count = jnp.minimum(t + 1, w).astype(jnp.float32)[None, :, None]
        outs.append((cg[:, 1:] - lower) / count - pf[:, :, sl])
    pooled = jnp.stack(outs, axis=2).astype(p.dtype)
    y = jnp.einsum('bsgc,gce->bsge', pooled, w_pool).reshape(b, s, POOL_WIDTH)
    return y * pool_scale


def conv_glu_ffn(h, w_up, conv_w, conv_b, w_down):
    u = h @ w_up
    s = u.shape[1]
    up = jnp.pad(u, ((0, 0), (CONV_WIDTH - 1, 0), (0, 0)))
    uc = conv_b + conv_w[0] * up[:, 0:s]
    for j in range(1, CONV_WIDTH):
        uc = uc + conv_w[j] * up[:, j:j + s]
    a, g = jnp.split(uc, 2, axis=-1)
    return (jax.nn.silu(a) * g) @ w_down


def setup_inputs(seed: int = 0) -> dict:
    key = jax.random.key(seed)
    ks = jax.random.split(key, 20)
    f32 = jnp.float32
    L, D = DEPTH, D_MODEL

    def nrm(k, shape, fan_in):
        return jax.random.normal(k, shape, f32) * (fan_in ** -0.5)

    def gain(k, shape):
        return 1.0 + 0.05 * jax.random.normal(k, shape, f32)

    return {
        "x": jax.random.normal(ks[0], (BATCH, SEQ, D), f32),
        "c": jax.random.normal(ks[1], (BATCH, D), f32),
        "w_ada": nrm(ks[2], (L, D, 6 * D), D),
        "b_ada": 0.02 * jax.random.normal(ks[3], (L, 6 * D), f32),
        "g_pre_mix": gain(ks[4], (L, D)),
        "w_in": nrm(ks[5], (L, D, IN_WIDTH), D),
        "w_pool": nrm(ks[6], (L, POOL_GROUPS, POOL_GROUP_DIM, POOL_GROUP_DIM), POOL_GROUP_DIM),
        "pool_scale": gain(ks[7], (L, POOL_WIDTH)),
        "attn_out_gain": gain(ks[8], (L, ATTN_WIDTH)),
        "pool_out_gain": gain(ks[9], (L, POOL_WIDTH)),
        "w_out": nrm(ks[10], (L, MIX_WIDTH, D), MIX_WIDTH),
        "g_post_mix": gain(ks[11], (L, D)),
        "g_pre_ffn": gain(ks[12], (L, D)),
        "w_up": nrm(ks[13], (L, D, 2 * D_FF), D),
        "conv_w": nrm(ks[14], (L, CONV_WIDTH, 2 * D_FF), CONV_WIDTH),
        "conv_b": 0.02 * jax.random.normal(ks[15], (L, 2 * D_FF), f32),
        "w_down": nrm(ks[16], (L, D_FF, D), D_FF),
        "g_post_ffn": gain(ks[17], (L, D)),
    }


def reference(x, c, w_ada, b_ada, g_pre_mix, w_in, w_pool, pool_scale, attn_out_gain,
              pool_out_gain, w_out, g_post_mix, g_pre_ffn, w_up, conv_w, conv_b, w_down,
              g_post_ffn):
    b, s, _ = x.shape
    pos = jnp.arange(s, dtype=jnp.float32)
    inv_freq = jnp.power(jnp.float32(ROPE_THETA),
                         -jnp.arange(0, ROT_DIM, 2, dtype=jnp.float32) / ROT_DIM)
    ang = pos[:, None] * inv_freq[None, :]
    cos = jnp.cos(ang).astype(x.dtype)
    sin = jnp.sin(ang).astype(x.dtype)
    c_act = jax.nn.silu(c)

    def heads(t):
        return t.reshape(b, s, ATTN_HEADS, HEAD_DIM).transpose(0, 2, 1, 3)

    for l in range(DEPTH):
        mod = c_act @ w_ada[l] + b_ada[l]
        sh1, sc1, gt1, sh2, sc2, gt2 = jnp.split(mod, 6, axis=-1)

        h = rms_norm(x, g_pre_mix[l]) * (1.0 + sc1[:, None]) + sh1[:, None]
        z = h @ w_in[l]
        q = z[..., 0:ATTN_WIDTH]
        k = z[..., ATTN_WIDTH:2 * ATTN_WIDTH]
        v = z[..., 2 * ATTN_WIDTH:3 * ATTN_WIDTH]
        pz = z[..., 3 * ATTN_WIDTH:]
        qh = partial_rotary(heads(q), cos, sin)
        kh = partial_rotary(heads(k), cos, sin)
        ao = moba_attention(qh, kh, heads(v)).transpose(0, 2, 1, 3).reshape(b, s, ATTN_WIDTH)
        po = multiscale_pool(pz, w_pool[l], pool_scale[l])
        merged = jnp.concatenate([rms_norm(ao, attn_out_gain[l]),
                                  rms_norm(po, pool_out_gain[l])], axis=-1)
        y = merged @ w_out[l]
        x = x + gt1[:, None] * rms_norm(y, g_post_mix[l])

        h = rms_norm(x, g_pre_ffn[l]) * (1.0 + sc2[:, None]) + sh2[:, None]
        y = conv_glu_ffn(h, w_up[l], conv_w[l], conv_b[l], w_down[l])
        x = x + gt2[:, None] * rms_norm(y, g_post_ffn[l])
    return x
```

```python
import functools

import jax
import jax.numpy as jnp
from jax import lax
from jax.experimental import pallas as pl
from jax.experimental.pallas import tpu as pltpu

F32 = jnp.float32
BF16 = jnp.bfloat16

HEAD_DIM = 64
HEADS_PER_GROUP = 2
LANES = 128
MOBA_BLOCK = 256
MOBA_TOP_K = 3
POOL_WINDOWS = (2, 4, 8, 16)
POOL_HALO = 16
ROT_DIM = HEAD_DIM // 4
ROPE_THETA = 500000.0
CONV_WIDTH = 3
CONV_HALO = 8
NORM_EPS = 1e-6
NEG_INF = -1e30
VMEM_LIMIT = 48 * 1024 * 1024

_NT = (((1,), (1,)), ((), ()))


def _rms(x):
    return x * lax.rsqrt(jnp.mean(x * x, axis=-1, keepdims=True) + NORM_EPS)


def _params(*sem):
    return pltpu.CompilerParams(dimension_semantics=sem, vmem_limit_bytes=VMEM_LIMIT)


def _mod_kernel(c_ref, w_ref, b_ref, o_ref):
    c = c_ref[...]
    c_act = c * jax.nn.sigmoid(c)
    o_ref[0] = jnp.dot(c_act.astype(BF16), w_ref[0].astype(BF16),
                       preferred_element_type=F32) + b_ref[0]


def _mod_call(c, w_ada, b_ada, tn=1536):
    nl, d, n6 = w_ada.shape
    b = c.shape[0]
    return pl.pallas_call(
        _mod_kernel,
        grid=(nl, n6 // tn),
        in_specs=[pl.BlockSpec((b, d), lambda l, j: (0, 0)),
                  pl.BlockSpec((1, d, tn), lambda l, j: (l, 0, j)),
                  pl.BlockSpec((1, 1, tn), lambda l, j: (l, 0, j))],
        out_specs=pl.BlockSpec((1, b, tn), lambda l, j: (l, 0, j)),
        out_shape=jax.ShapeDtypeStruct((nl, b, n6), F32),
        compiler_params=_params("arbitrary", "arbitrary"),
        name="adaln_mod",
    )(c, w_ada, b_ada.reshape(nl, 1, n6))


def _in_kernel(x_ref, sc_ref, sh_ref, g_ref, w_ref, cos_ref, sa_ref, sb_ref, wp_ref, ps_ref,
               pg_ref, q_ref, k_ref, vt_ref, pn_ref, halo_ref, *, ts, aw):
    si = pl.program_id(1)
    h = _rms(x_ref[0]) * g_ref[...]
    h = h * (1.0 + sc_ref[0]) + sh_ref[0]
    z = jnp.dot(h.astype(BF16), w_ref[...], preferred_element_type=F32)

    cos = cos_ref[...]
    sa = sa_ref[...]
    sb = sb_ref[...]

    def rot(t):
        return t * cos + pltpu.roll(t, LANES - ROT_DIM // 2, 1) * sa + pltpu.roll(t, ROT_DIM // 2, 1) * sb

    scale = HEAD_DIM ** -0.5
    ngroups = aw // LANES
    for gi in range(ngroups):
        lo = gi * LANES
        q_ref[0, :, lo:lo + LANES] = (rot(z[:, lo:lo + LANES]) * scale).astype(BF16)
        k_ref[0, :, lo:lo + LANES] = rot(z[:, aw + lo:aw + lo + LANES]).astype(BF16)
        vt = z[:, 2 * aw + lo:2 * aw + lo + LANES].T
        for jb in range(ts // MOBA_BLOCK):
            vt_ref[0, gi, jb] = vt[:, jb * MOBA_BLOCK:(jb + 1) * MOBA_BLOCK].astype(BF16)

    pz = z[:, 3 * aw:]

    @pl.when(si == 0)
    def _():
        halo_ref[0:POOL_HALO, :] = jnp.zeros((POOL_HALO, pz.shape[1]), F32)

    halo_ref[POOL_HALO:POOL_HALO + ts, :] = pz
    tpos = si * ts + lax.broadcasted_iota(jnp.int32, (ts, 1), 0)
    outs = []
    for g, w in enumerate(POOL_WINDOWS):
        lo = g * LANES
        s = pz[:, lo:lo + LANES]
        for i in range(1, w):
            s = s + halo_ref[POOL_HALO - i:POOL_HALO - i + ts, lo:lo + LANES]
        cnt = jnp.minimum(tpos + 1, w).astype(F32)
        pooled = s / cnt - pz[:, lo:lo + LANES]
        outs.append(jnp.dot(pooled.astype(BF16), wp_ref[g], preferred_element_type=F32))
    halo_ref[0:POOL_HALO, :] = pz[ts - POOL_HALO:, :]
    po = jnp.concatenate(outs, axis=1) * ps_ref[...]
    pn_ref[0] = (_rms(po) * pg_ref[...]).astype(BF16)


def _in_call(x, sc, sh, g, w_in, cos_t, sa_t, sb_t, w_pool, pool_scale, pool_gain, ts=512):
    b, s, d = x.shape
    n_in = w_in.shape[1]
    pw = w_pool.shape[0] * w_pool.shape[1]
    aw = (n_in - pw) // 3
    ngroups = aw // LANES
    nb = s // MOBA_BLOCK
    tok = lambda bi, si: (bi, si, 0)
    row = lambda bi, si: (bi, 0, 0)
    fix2 = lambda bi, si: (0, 0)
    return pl.pallas_call(
        functools.partial(_in_kernel, ts=ts, aw=aw),
        grid=(b, s // ts),
        in_specs=[pl.BlockSpec((1, ts, d), tok),
                  pl.BlockSpec((1, 1, d), row),
                  pl.BlockSpec((1, 1, d), row),
                  pl.BlockSpec((1, d), fix2),
                  pl.BlockSpec((d, n_in), fix2),
                  pl.BlockSpec((ts, LANES), lambda bi, si: (si, 0)),
                  pl.BlockSpec((ts, LANES), lambda bi, si: (si, 0)),
                  pl.BlockSpec((ts, LANES), lambda bi, si: (si, 0)),
                  pl.BlockSpec(w_pool.shape, lambda bi, si: (0, 0, 0)),
                  pl.BlockSpec((1, pw), fix2),
                  pl.BlockSpec((1, pw), fix2)],
        out_specs=[pl.BlockSpec((1, ts, aw), tok),
                   pl.BlockSpec((1, ts, aw), tok),
                   pl.BlockSpec((1, ngroups, ts // MOBA_BLOCK, LANES, MOBA_BLOCK),
                                lambda bi, si: (bi, 0, si, 0, 0)),
                   pl.BlockSpec((1, ts, pw), tok)],
        out_shape=[jax.ShapeDtypeStruct((b, s, aw), BF16),
                   jax.ShapeDtypeStruct((b, s, aw), BF16),
                   jax.ShapeDtypeStruct((b, ngroups, nb, LANES, MOBA_BLOCK), BF16),
                   jax.ShapeDtypeStruct((b, s, pw), BF16)],
        scratch_shapes=[pltpu.VMEM((POOL_HALO + ts, pw), F32)],
        compiler_params=_params("arbitrary", "arbitrary"),
        name="in_proj",
    )(x, sc, sh, g, w_in, cos_t, sa_t, sb_t, w_pool, pool_scale, pool_gain)


def _attn_kernel(q_ref, k_ref, vt_ref, o_ref, km_ref, bias_ref, *, nb):
    n = pl.program_id(2)
    blk = MOBA_BLOCK

    @pl.when(n == 0)
    def _():
        for j in range(nb):
            kj = k_ref[0, j * blk:(j + 1) * blk, :].astype(F32)
            km_ref[j:j + 1, :] = jnp.mean(kj, axis=0, keepdims=True)

    q2 = q_ref[0]
    lane = lax.broadcasted_iota(jnp.int32, (1, LANES), 1)
    km = km_ref[...]
    km_hi = km.astype(BF16)
    km_lo = (km - km_hi.astype(F32)).astype(BF16)
    jidx = lax.broadcasted_iota(jnp.int32, (nb, 1), 0)
    past = jidx < n
    causal = (lax.broadcasted_iota(jnp.int32, (blk, blk), 0)
              <= lax.broadcasted_iota(jnp.int32, (blk, blk), 1))
    kd = k_ref[0, pl.ds(pl.multiple_of(n * blk, blk), blk), :]
    vd = vt_ref[0, 0, n]
    outs = []
    for hh in range(HEADS_PER_GROUP):
        in_head = (lane >= hh * HEAD_DIM) & (lane < (hh + 1) * HEAD_DIM)
        qh = jnp.where(in_head, q2, jnp.zeros_like(q2))

        gate = (lax.dot_general(km_hi, qh, _NT, preferred_element_type=F32)
                + lax.dot_general(km_lo, qh, _NT, preferred_element_type=F32))
        gate = jnp.where(past, gate, NEG_INF)
        rank = jnp.zeros(gate.shape, jnp.int32)
        for i in range(nb):
            gi = gate[i:i + 1, :]
            beats = (gi > gate) | ((gi == gate) & (i < jidx))
            rank = rank + beats.astype(jnp.int32)
        sel = (rank < MOBA_TOP_K) & past
        bias = jnp.where(sel, 0.0, NEG_INF).astype(F32)
        for j in range(nb):
            bias_ref[j] = bias[j:j + 1, :]

        s = lax.dot_general(kd, qh, _NT, preferred_element_type=F32)
        s = jnp.where(causal, s, NEG_INF)
        m = jnp.max(s, axis=0, keepdims=True)
        p = jnp.exp(s - m)
        l = jnp.sum(p, axis=0, keepdims=True)
        acc = jnp.dot(vd, p.astype(BF16), preferred_element_type=F32)

        def body(j, carry):
            m, l, acc = carry
            kj = k_ref[0, pl.ds(pl.multiple_of(j * blk, blk), blk), :]
            s = lax.dot_general(kj, qh, _NT, preferred_element_type=F32) + bias_ref[j]
            m_new = jnp.maximum(m, jnp.max(s, axis=0, keepdims=True))
            alpha = jnp.exp(m - m_new)
            p = jnp.exp(s - m_new)
            l = alpha * l + jnp.sum(p, axis=0, keepdims=True)
            acc = alpha * acc + jnp.dot(vt_ref[0, 0, j], p.astype(BF16),
                                        preferred_element_type=F32)
            return m_new, l, acc

        m, l, acc = lax.fori_loop(0, n, body, (m, l, acc))
        outs.append(acc / l)

    rowi = lax.broadcasted_iota(jnp.int32, (LANES, 1), 0)
    o_t = jnp.where(rowi < HEAD_DIM, outs[0], outs[1])
    o_ref[0] = o_t.T


def _attn_call(q, k, vt):
    b, s, aw = q.shape
    ngroups = aw // LANES
    nb = s // MOBA_BLOCK
    return pl.pallas_call(
        functools.partial(_attn_kernel, nb=nb),
        grid=(b, ngroups, nb),
        in_specs=[pl.BlockSpec((1, MOBA_BLOCK, LANES), lambda bi, gi, n: (bi, n, gi)),
                  pl.BlockSpec((1, s, LANES), lambda bi, gi, n: (bi, 0, gi)),
                  pl.BlockSpec((1, 1, nb, LANES, MOBA_BLOCK), lambda bi, gi, n: (bi, gi, 0, 0, 0))],
        out_specs=pl.BlockSpec((1, MOBA_BLOCK, LANES), lambda bi, gi, n: (bi, n, gi)),
        out_shape=jax.ShapeDtypeStruct((b, s, aw), F32),
        scratch_shapes=[pltpu.VMEM((nb, LANES), F32),
                        pltpu.VMEM((nb, 1, MOBA_BLOCK), F32)],
        compiler_params=_params("arbitrary", "arbitrary", "arbitrary"),
        name="moba_attn",
    )(q, k, vt)


def _out_kernel(ao_ref, pn_ref, x_ref, ga_ref, w_ref, gp_ref, gt_ref, o_ref, *, aw):
    an = (_rms(ao_ref[0]) * ga_ref[...]).astype(BF16)
    y = (jnp.dot(an, w_ref[0:aw, :], preferred_element_type=F32)
         + jnp.dot(pn_ref[0], w_ref[aw:, :], preferred_element_type=F32))
    o_ref[0] = x_ref[0] + gt_ref[0] * (_rms(y) * gp_ref[...])


def _out_call(ao, pn, x, attn_gain, w_out, g_post, gt, ts=512):
    b, s, d = x.shape
    aw = ao.shape[2]
    pw = pn.shape[2]
    tok = lambda bi, si: (bi, si, 0)
    fix2 = lambda bi, si: (0, 0)
    return pl.pallas_call(
        functools.partial(_out_kernel, aw=aw),
        grid=(b, s // ts),
        in_specs=[pl.BlockSpec((1, ts, aw), tok),
                  pl.BlockSpec((1, ts, pw), tok),
                  pl.BlockSpec((1, ts, d), tok),
                  pl.BlockSpec((1, aw), fix2),
                  pl.BlockSpec(w_out.shape, fix2),
                  pl.BlockSpec((1, d), fix2),
                  pl.BlockSpec((1, 1, d), lambda bi, si: (bi, 0, 0))],
        out_specs=pl.BlockSpec((1, ts, d), tok),
        out_shape=jax.ShapeDtypeStruct((b, s, d), F32),
        compiler_params=_params("arbitrary", "arbitrary"),
        name="out_proj",
    )(ao, pn, x, attn_gain, w_out, g_post, gt)


def _ffn_kernel(x_ref, sc_ref, sh_ref, g_ref, wa_ref, wg_ref, cwa_ref, cwg_ref, cba_ref, cbg_ref,
                wd_ref, gp_ref, gt_ref, o_ref, h_ref, acc_ref, tail_ref, ubuf_ref, *, ts, tf):
    si = pl.program_id(1)
    fi = pl.program_id(2)
    nf = pl.num_programs(2)

    @pl.when(fi == 0)
    def _():
        h = _rms(x_ref[0]) * g_ref[...]
        h_ref[...] = (h * (1.0 + sc_ref[0]) + sh_ref[0]).astype(BF16)
        acc_ref[...] = jnp.zeros(acc_ref.shape, F32)

    def conv(w_ref, cw_ref, cb_ref, ti):
        u = jnp.dot(h_ref[...], w_ref[...], preferred_element_type=F32)

        @pl.when(si == 0)
        def _():
            tail_ref[ti] = jnp.zeros((CONV_HALO, tf), F32)

        ubuf_ref[0:CONV_HALO, :] = tail_ref[ti]
        ubuf_ref[CONV_HALO:CONV_HALO + ts, :] = u
        tail_ref[ti] = u[ts - CONV_HALO:, :]
        cw = cw_ref[0]
        uc = cb_ref[0] + cw[CONV_WIDTH - 1:CONV_WIDTH, :] * u
        for j in range(CONV_WIDTH - 1):
            back = CONV_WIDTH - 1 - j
            uc = uc + cw[j:j + 1, :] * ubuf_ref[CONV_HALO - back:CONV_HALO - back + ts, :]
        return uc

    a = conv(wa_ref, cwa_ref, cba_ref, fi)
    g = conv(wg_ref, cwg_ref, cbg_ref, nf + fi)
    act = (a * jax.nn.sigmoid(a) * g).astype(BF16)
    acc_ref[...] += jnp.dot(act, wd_ref[...], preferred_element_type=F32)

    @pl.when(fi == nf - 1)
    def _():
        o_ref[0] = x_ref[0] + gt_ref[0] * (_rms(acc_ref[...]) * gp_ref[...])


def _ffn_call(x, sc, sh, g_pre, w_up, conv_w, conv_b, w_down, g_post, gt, ts=1024, tf=256):
    b, s, d = x.shape
    f = w_down.shape[0]
    nf = f // tf
    cw = conv_w.reshape(CONV_WIDTH, 2 * nf, tf).transpose(1, 0, 2)
    cb = conv_b.reshape(2 * nf, 1, tf)
    tok = lambda bi, si, fi: (bi, si, 0)
    row = lambda bi, si, fi: (bi, 0, 0)
    fix2 = lambda bi, si, fi: (0, 0)
    return pl.pallas_call(
        functools.partial(_ffn_kernel, ts=ts, tf=tf),
        grid=(b, s // ts, nf),
        in_specs=[pl.BlockSpec((1, ts, d), tok),
                  pl.BlockSpec((1, 1, d), row),
                  pl.BlockSpec((1, 1, d), row),
                  pl.BlockSpec((1, d), fix2),
                  pl.BlockSpec((d, tf), lambda bi, si, fi: (0, fi)),
                  pl.BlockSpec((d, tf), lambda bi, si, fi: (0, nf + fi)),
                  pl.BlockSpec((1, CONV_WIDTH, tf), lambda bi, si, fi: (fi, 0, 0)),
                  pl.BlockSpec((1, CONV_WIDTH, tf), lambda bi, si, fi: (nf + fi, 0, 0)),
                  pl.BlockSpec((1, 1, tf), lambda bi, si, fi: (fi, 0, 0)),
                  pl.BlockSpec((1, 1, tf), lambda bi, si, fi: (nf + fi, 0, 0)),
                  pl.BlockSpec((tf, d), lambda bi, si, fi: (fi, 0)),
                  pl.BlockSpec((1, d), fix2),
                  pl.BlockSpec((1, 1, d), row)],
        out_specs=pl.BlockSpec((1, ts, d), tok),
        out_shape=jax.ShapeDtypeStruct((b, s, d), F32),
        scratch_shapes=[pltpu.VMEM((ts, d), BF16),
                        pltpu.VMEM((ts, d), F32),
                        pltpu.VMEM((2 * nf, CONV_HALO, tf), F32),
                        pltpu.VMEM((CONV_HALO + ts, tf), F32)],
        compiler_params=_params("arbitrary", "arbitrary", "arbitrary"),
        name="conv_ffn",
    )(x, sc, sh, g_pre, w_up, w_up, cw, cw, cb, cb, w_down, g_post, gt)


def _rotary_tables(s):
    half = ROT_DIM // 2
    pos = jnp.arange(s, dtype=F32)
    inv_freq = jnp.power(jnp.float32(ROPE_THETA), -jnp.arange(0, ROT_DIM, 2, dtype=F32) / ROT_DIM)
    ang = pos[:, None] * inv_freq[None, :]
    cos = jnp.cos(ang)
    sin = jnp.sin(ang)
    ones = jnp.ones((s, HEAD_DIM - ROT_DIM), F32)
    zeros = jnp.zeros((s, HEAD_DIM - ROT_DIM), F32)
    zh = jnp.zeros((s, half), F32)
    cos_h = jnp.concatenate([cos, cos, ones], axis=1)
    sa_h = jnp.concatenate([-sin, zh, zeros], axis=1)
    sb_h = jnp.concatenate([zh, sin, zeros], axis=1)
    rep = lambda t: jnp.concatenate([t] * HEADS_PER_GROUP, axis=1)
    return rep(cos_h), rep(sa_h), rep(sb_h)


def kernel(x, c, w_ada, b_ada, g_pre_mix, w_in, w_pool, pool_scale, attn_out_gain, pool_out_gain,
           w_out, g_post_mix, g_pre_ffn, w_up, conv_w, conv_b, w_down, g_post_ffn):
    b, s, d = x.shape
    depth = w_ada.shape[0]
    assert s % MOBA_BLOCK == 0 and d % LANES == 0
    mod = _mod_call(c, w_ada, b_ada)
    cos_t, sa_t, sb_t = _rotary_tables(s)
    for l in range(depth):
        sh1, sc1, gt1, sh2, sc2, gt2 = (mod[l, :, i * d:(i + 1) * d].reshape(b, 1, d)
                                        for i in range(6))
        q, k, vt, pn = _in_call(x, sc1, sh1, g_pre_mix[l][None], w_in[l].astype(BF16),
                                cos_t, sa_t, sb_t, w_pool[l].astype(BF16),
                                pool_scale[l][None], pool_out_gain[l][None])
        ao = _attn_call(q, k, vt)
        x = _out_call(ao, pn, x, attn_out_gain[l][None], w_out[l].astype(BF16),
                      g_post_mix[l][None], gt1)
        x = _ffn_call(x, sc2, sh2, g_pre_ffn[l][None], w_up[l].astype(BF16), conv_w[l], conv_b[l],
                      w_down[l].astype(BF16), g_post_ffn[l][None], gt2)
    return x
```

```python
import functools

import jax
import jax.numpy as jnp
from jax import lax
from jax.experimental import pallas as pl
from jax.experimental.pallas import tpu as pltpu

F32 = jnp.float32
BF16 = jnp.bfloat16

HEAD_DIM = 64
HEADS_PER_GROUP = 2
LANES = 128
MOBA_BLOCK = 256
MOBA_TOP_K = 3
POOL_WINDOWS = (2, 4, 8, 16)
POOL_HALO = 16
ROT_DIM = HEAD_DIM // 4
ROPE_THETA = 500000.0
CONV_WIDTH = 3
CONV_HALO = 8
NORM_EPS = 1e-6
NEG_INF = -1e30
LOG2_E = 1.4426950408889634
VT_ROWS = HEAD_DIM + 16
PAIR_KEYS = 2 * MOBA_BLOCK
VMEM_LIMIT = 48 * 1024 * 1024

_NT = (((1,), (1,)), ((), ()))


def _rms(x):
    return x * lax.rsqrt(jnp.mean(x * x, axis=-1, keepdims=True) + NORM_EPS)


def _params(*sem):
    return pltpu.CompilerParams(dimension_semantics=sem, vmem_limit_bytes=VMEM_LIMIT)


def _mod_kernel(c_ref, w_ref, b_ref, o_ref):
    c = c_ref[...]
    c_act = c * jax.nn.sigmoid(c)
    o_ref[0] = jnp.dot(c_act.astype(BF16), w_ref[0].astype(BF16),
                       preferred_element_type=F32) + b_ref[0]


def _mod_call(c, w_ada, b_ada, tn=1536):
    nl, d, n6 = w_ada.shape
    b = c.shape[0]
    return pl.pallas_call(
        _mod_kernel,
        grid=(nl, n6 // tn),
        in_specs=[pl.BlockSpec((b, d), lambda l, j: (0, 0)),
                  pl.BlockSpec((1, d, tn), lambda l, j: (l, 0, j)),
                  pl.BlockSpec((1, 1, tn), lambda l, j: (l, 0, j))],
        out_specs=pl.BlockSpec((1, b, tn), lambda l, j: (l, 0, j)),
        out_shape=jax.ShapeDtypeStruct((nl, b, n6), F32),
        compiler_params=_params("arbitrary", "arbitrary"),
        name="adaln_mod",
    )(c, w_ada, b_ada.reshape(nl, 1, n6))


def _in_kernel(x_ref, sc_ref, sh_ref, g_ref, w_ref, cos_ref, sa_ref, sb_ref, wp_ref, ps_ref,
               pg_ref, q_ref, k_ref, vt_ref, pn_ref, halo_ref, *, ts, aw):
    si = pl.program_id(1)
    h = _rms(x_ref[0]) * g_ref[...]
    h = h * (1.0 + sc_ref[0]) + sh_ref[0]
    z = jnp.dot(h.astype(BF16), w_ref[...], preferred_element_type=F32)

    cos = cos_ref[...]
    sa = sa_ref[...]
    sb = sb_ref[...]

    def rot(t):
        return t * cos + pltpu.roll(t, LANES - ROT_DIM // 2, 1) * sa + pltpu.roll(t, ROT_DIM // 2, 1) * sb

    scale = HEAD_DIM ** -0.5 * LOG2_E
    ngroups = aw // LANES
    ones = jnp.ones((VT_ROWS - HEAD_DIM, PAIR_KEYS), BF16)
    for gi in range(ngroups):
        lo = gi * LANES
        q_ref[0, :, lo:lo + LANES] = (rot(z[:, lo:lo + LANES]) * scale).astype(BF16)
        k_ref[0, :, lo:lo + LANES] = rot(z[:, aw + lo:aw + lo + LANES]).astype(BF16)
        vt = z[:, 2 * aw + lo:2 * aw + lo + LANES].T.astype(BF16)
        for hh in range(HEADS_PER_GROUP):
            head = gi * HEADS_PER_GROUP + hh
            for jp in range(ts // PAIR_KEYS):
                vt_ref[0, head, jp, 0:HEAD_DIM, :] = vt[hh * HEAD_DIM:(hh + 1) * HEAD_DIM,
                                                        jp * PAIR_KEYS:(jp + 1) * PAIR_KEYS]
                vt_ref[0, head, jp, HEAD_DIM:VT_ROWS, :] = ones

    pz = z[:, 3 * aw:]

    @pl.when(si == 0)
    def _():
        halo_ref[0:POOL_HALO, :] = jnp.zeros((POOL_HALO, pz.shape[1]), F32)

    halo_ref[POOL_HALO:POOL_HALO + ts, :] = pz
    tpos = si * ts + lax.broadcasted_iota(jnp.int32, (ts, 1), 0)
    outs = []
    for g, w in enumerate(POOL_WINDOWS):
        lo = g * LANES
        s = pz[:, lo:lo + LANES]
        for i in range(1, w):
            s = s + halo_ref[POOL_HALO - i:POOL_HALO - i + ts, lo:lo + LANES]
        cnt = jnp.minimum(tpos + 1, w).astype(F32)
        pooled = s / cnt - pz[:, lo:lo + LANES]
        outs.append(jnp.dot(pooled.astype(BF16), wp_ref[g], preferred_element_type=F32))
    halo_ref[0:POOL_HALO, :] = pz[ts - POOL_HALO:, :]
    po = jnp.concatenate(outs, axis=1) * ps_ref[...]
    pn_ref[0] = (_rms(po) * pg_ref[...]).astype(BF16)


def _in_call(x, sc, sh, g, w_in, cos_t, sa_t, sb_t, w_pool, pool_scale, pool_gain, ts=512):
    b, s, d = x.shape
    n_in = w_in.shape[1]
    pw = w_pool.shape[0] * w_pool.shape[1]
    aw = (n_in - pw) // 3
    nheads = aw // HEAD_DIM
    nb = s // MOBA_BLOCK
    tok = lambda bi, si: (bi, si, 0)
    row = lambda bi, si: (bi, 0, 0)
    fix2 = lambda bi, si: (0, 0)
    return pl.pallas_call(
        functools.partial(_in_kernel, ts=ts, aw=aw),
        grid=(b, s // ts),
        in_specs=[pl.BlockSpec((1, ts, d), tok),
                  pl.BlockSpec((1, 1, d), row),
                  pl.BlockSpec((1, 1, d), row),
                  pl.BlockSpec((1, d), fix2),
                  pl.BlockSpec((d, n_in), fix2),
                  pl.BlockSpec((ts, LANES), lambda bi, si: (si, 0)),
                  pl.BlockSpec((ts, LANES), lambda bi, si: (si, 0)),
                  pl.BlockSpec((ts, LANES), lambda bi, si: (si, 0)),
                  pl.BlockSpec(w_pool.shape, lambda bi, si: (0, 0, 0)),
                  pl.BlockSpec((1, pw), fix2),
                  pl.BlockSpec((1, pw), fix2)],
        out_specs=[pl.BlockSpec((1, ts, aw), tok),
                   pl.BlockSpec((1, ts, aw), tok),
                   pl.BlockSpec((1, nheads, ts // PAIR_KEYS, VT_ROWS, PAIR_KEYS),
                                lambda bi, si: (bi, 0, si, 0, 0)),
                   pl.BlockSpec((1, ts, pw), tok)],
        out_shape=[jax.ShapeDtypeStruct((b, s, aw), BF16),
                   jax.ShapeDtypeStruct((b, s, aw), BF16),
                   jax.ShapeDtypeStruct((b, nheads, s // PAIR_KEYS, VT_ROWS, PAIR_KEYS), BF16),
                   jax.ShapeDtypeStruct((b, s, pw), BF16)],
        scratch_shapes=[pltpu.VMEM((POOL_HALO + ts, pw), F32)],
        compiler_params=_params("arbitrary", "arbitrary"),
        name="in_proj",
    )(x, sc, sh, g, w_in, cos_t, sa_t, sb_t, w_pool, pool_scale, pool_gain)


def _attn_kernel(q_ref, k_ref, vt_ref, o_ref, km_ref, bias_ref, s0_ref, s1_ref, p0_ref, p1_ref,
                 *, nb):
    s_slots = (s0_ref, s1_ref)
    p_slots = (p0_ref, p1_ref)
    n = pl.program_id(2)
    blk = MOBA_BLOCK

    @pl.when(n == 0)
    def _():
        for j in range(nb):
            kj = k_ref[0, j * blk:(j + 1) * blk, :].astype(F32)
            km_ref[j:j + 1, :] = jnp.mean(kj, axis=0, keepdims=True)

    q2 = q_ref[0]
    lane = lax.broadcasted_iota(jnp.int32, (1, LANES), 1)
    km = km_ref[...]
    km_hi = km.astype(BF16)
    km_lo = (km - km_hi.astype(F32)).astype(BF16)
    jidx = lax.broadcasted_iota(jnp.int32, (nb, 1), 0)
    past = jidx < n
    heads = range(HEADS_PER_GROUP)

    qh = []
    for hh in heads:
        in_head = (lane >= hh * HEAD_DIM) & (lane < (hh + 1) * HEAD_DIM)
        qh.append(jnp.where(in_head, q2, jnp.zeros_like(q2)))

        gate = (lax.dot_general(km_hi, qh[hh], _NT, preferred_element_type=F32)
                + lax.dot_general(km_lo, qh[hh], _NT, preferred_element_type=F32))
        gate = jnp.where(past, gate, NEG_INF)
        rank = jnp.zeros(gate.shape, jnp.int32)
        for i in range(nb):
            gi = gate[i:i + 1, :]
            beats = (gi > gate) | ((gi == gate) & (i < jidx))
            rank = rank + beats.astype(jnp.int32)
        sel = (rank < MOBA_TOP_K) & past
        bias = jnp.where(sel, 0.0, NEG_INF).astype(F32)
        for j in range(nb):
            bias_ref[hh, j] = bias[j:j + 1, :]

    causal = (lax.broadcasted_iota(jnp.int32, (blk, blk), 0)
              <= lax.broadcasted_iota(jnp.int32, (blk, blk), 1))
    kd = k_ref[0, pl.ds(pl.multiple_of(n * blk, blk), blk), :]

    npairs = nb // 2
    trips = (n + 1) // 2

    def issue_scores(pi, slot):
        kp = k_ref[0, pl.ds(pl.multiple_of(pi * (2 * blk), 2 * blk), 2 * blk), :]
        for hh in heads:
            s_slots[slot][hh] = lax.dot_general(kp, qh[hh], _NT, preferred_element_type=F32)

    def weighted_values(hh, pi, slot):
        return jnp.dot(vt_ref[0, hh, pi], p_slots[slot][hh], preferred_element_type=F32)

    own_half = pl.multiple_of((n % 2) * blk, blk)
    other_half = pl.multiple_of((1 - n % 2) * blk, blk)
    init = []
    for hh in heads:
        s = lax.dot_general(kd, qh[hh], _NT, preferred_element_type=F32)
        s = jnp.where(causal, s, NEG_INF)
        m = jnp.max(s, axis=0, keepdims=True)
        p_slots[1][hh, pl.ds(own_half, blk), :] = jnp.exp2(s - m).astype(BF16)
        p_slots[1][hh, pl.ds(other_half, blk), :] = jnp.zeros((blk, blk), BF16)
        init.append((m, jnp.ones_like(m), jnp.zeros((VT_ROWS, blk), F32)))
    issue_scores(0, 0)

    def trip(i, carry, slot):
        issue_scores(jnp.minimum(i + 1, npairs - 1), 1 - slot)
        pending = jnp.where(i == 0, n // 2, i - 1)
        accs = [carry[hh][1] * carry[hh][2] + weighted_values(hh, pending, 1 - slot) for hh in heads]
        out = []
        for hh in heads:
            m = carry[hh][0]
            b0 = bias_ref[hh, 2 * i]
            b1 = bias_ref[hh, 2 * i + 1]
            s0 = s_slots[slot][hh, 0:blk, :]
            s1 = s_slots[slot][hh, blk:2 * blk, :]
            m_new = jnp.maximum(m, jnp.maximum(jnp.max(s0, axis=0, keepdims=True) + b0,
                                               jnp.max(s1, axis=0, keepdims=True) + b1))
            p_slots[slot][hh, 0:blk, :] = jnp.exp2(s0 - (m_new - b0)).astype(BF16)
            p_slots[slot][hh, blk:2 * blk, :] = jnp.exp2(s1 - (m_new - b1)).astype(BF16)
            out.append((m_new, jnp.exp2(m - m_new), accs[hh]))
        return tuple(out)

    def body(i, carry):
        return lax.cond(i % 2 == 0, lambda: trip(i, carry, 0), lambda: trip(i, carry, 1))

    carry = lax.fori_loop(0, trips, body, tuple(init))
    pending = jnp.where(trips == 0, n // 2, trips - 1)

    def drain(slot):
        return tuple(carry[hh][1] * carry[hh][2] + weighted_values(hh, pending, slot) for hh in heads)

    accs = lax.cond(trips % 2 == 1, lambda: drain(0), lambda: drain(1))
    outs = []
    for hh in heads:
        acc = accs[hh]
        outs.append(acc[0:HEAD_DIM, :] / acc[HEAD_DIM:HEAD_DIM + 1, :])
    o_ref[0] = jnp.concatenate(outs, axis=0).T


def _attn_call(q, k, vt):
    b, s, aw = q.shape
    ngroups = aw // LANES
    nb = s // MOBA_BLOCK
    return pl.pallas_call(
        functools.partial(_attn_kernel, nb=nb),
        grid=(b, ngroups, nb),
        in_specs=[pl.BlockSpec((1, MOBA_BLOCK, LANES), lambda bi, gi, n: (bi, n, gi)),
                  pl.BlockSpec((1, s, LANES), lambda bi, gi, n: (bi, 0, gi)),
                  pl.BlockSpec((1, HEADS_PER_GROUP, nb // 2, VT_ROWS, PAIR_KEYS),
                               lambda bi, gi, n: (bi, gi, 0, 0, 0))],
        out_specs=pl.BlockSpec((1, MOBA_BLOCK, LANES), lambda bi, gi, n: (bi, n, gi)),
        out_shape=jax.ShapeDtypeStruct((b, s, aw), F32),
        scratch_shapes=[pltpu.VMEM((nb, LANES), F32),
                        pltpu.VMEM((HEADS_PER_GROUP, nb, 1, MOBA_BLOCK), F32),
                        pltpu.VMEM((HEADS_PER_GROUP, PAIR_KEYS, MOBA_BLOCK), F32),
                        pltpu.VMEM((HEADS_PER_GROUP, PAIR_KEYS, MOBA_BLOCK), F32),
                        pltpu.VMEM((HEADS_PER_GROUP, PAIR_KEYS, MOBA_BLOCK), BF16),
                        pltpu.VMEM((HEADS_PER_GROUP, PAIR_KEYS, MOBA_BLOCK), BF16)],
        compiler_params=_params("arbitrary", "arbitrary", "arbitrary"),
        name="moba_attn",
    )(q, k, vt)


def _out_kernel(ao_ref, pn_ref, x_ref, ga_ref, w_ref, gp_ref, gt_ref, o_ref, *, aw):
    an = (_rms(ao_ref[0]) * ga_ref[...]).astype(BF16)
    y = (jnp.dot(an, w_ref[0:aw, :], preferred_element_type=F32)
         + jnp.dot(pn_ref[0], w_ref[aw:, :], preferred_element_type=F32))
    o_ref[0] = x_ref[0] + gt_ref[0] * (_rms(y) * gp_ref[...])


def _out_call(ao, pn, x, attn_gain, w_out, g_post, gt, ts=512):
    b, s, d = x.shape
    aw = ao.shape[2]
    pw = pn.shape[2]
    tok = lambda bi, si: (bi, si, 0)
    fix2 = lambda bi, si: (0, 0)
    return pl.pallas_call(
        functools.partial(_out_kernel, aw=aw),
        grid=(b, s // ts),
        in_specs=[pl.BlockSpec((1, ts, aw), tok),
                  pl.BlockSpec((1, ts, pw), tok),
                  pl.BlockSpec((1, ts, d), tok),
                  pl.BlockSpec((1, aw), fix2),
                  pl.BlockSpec(w_out.shape, fix2),
                  pl.BlockSpec((1, d), fix2),
                  pl.BlockSpec((1, 1, d), lambda bi, si: (bi, 0, 0))],
        out_specs=pl.BlockSpec((1, ts, d), tok),
        out_shape=jax.ShapeDtypeStruct((b, s, d), F32),
        compiler_params=_params("arbitrary", "arbitrary"),
        name="out_proj",
    )(ao, pn, x, attn_gain, w_out, g_post, gt)


def _ffn_kernel(x_ref, sc_ref, sh_ref, g_ref, wa_ref, wg_ref, cwa_ref, cwg_ref, cba_ref, cbg_ref,
                wd_ref, gp_ref, gt_ref, o_ref, h_ref, acc_ref, tail_ref, ubuf_ref, *, ts, tf):
    si = pl.program_id(1)
    fi = pl.program_id(2)
    nf = pl.num_programs(2)

    @pl.when(fi == 0)
    def _():
        h = _rms(x_ref[0]) * g_ref[...]
        h_ref[...] = (h * (1.0 + sc_ref[0]) + sh_ref[0]).astype(BF16)
        acc_ref[...] = jnp.zeros(acc_ref.shape, F32)

    def conv(w_ref, cw_ref, cb_ref, ti):
        u = jnp.dot(h_ref[...], w_ref[...], preferred_element_type=F32)

        @pl.when(si == 0)
        def _():
            tail_ref[ti] = jnp.zeros((CONV_HALO, tf), F32)

        ubuf_ref[0:CONV_HALO, :] = tail_ref[ti]
        ubuf_ref[CONV_HALO:CONV_HALO + ts, :] = u
        tail_ref[ti] = u[ts - CONV_HALO:, :]
        cw = cw_ref[0]
        uc = cb_ref[0] + cw[CONV_WIDTH - 1:CONV_WIDTH, :] * u
        for j in range(CONV_WIDTH - 1):
            back = CONV_WIDTH - 1 - j
            uc = uc + cw[j:j + 1, :] * ubuf_ref[CONV_HALO - back:CONV_HALO - back + ts, :]
        return uc

    a = conv(wa_ref, cwa_ref, cba_ref, fi)
    g = conv(wg_ref, cwg_ref, cbg_ref, nf + fi)
    act = (a * jax.nn.sigmoid(a) * g).astype(BF16)
    acc_ref[...] += jnp.dot(act, wd_ref[...], preferred_element_type=F32)

    @pl.when(fi == nf - 1)
    def _():
        o_ref[0] = x_ref[0] + gt_ref[0] * (_rms(acc_ref[...]) * gp_ref[...])


def _ffn_call(x, sc, sh, g_pre, w_up, conv_w, conv_b, w_down, g_post, gt, ts=1024, tf=256):
    b, s, d = x.shape
    f = w_down.shape[0]
    nf = f // tf
    cw = conv_w.reshape(CONV_WIDTH, 2 * nf, tf).transpose(1, 0, 2)
    cb = conv_b.reshape(2 * nf, 1, tf)
    tok = lambda bi, si, fi: (bi, si, 0)
    row = lambda bi, si, fi: (bi, 0, 0)
    fix2 = lambda bi, si, fi: (0, 0)
    return pl.pallas_call(
        functools.partial(_ffn_kernel, ts=ts, tf=tf),
        grid=(b, s // ts, nf),
        in_specs=[pl.BlockSpec((1, ts, d), tok),
                  pl.BlockSpec((1, 1, d), row),
                  pl.BlockSpec((1, 1, d), row),
                  pl.BlockSpec((1, d), fix2),
                  pl.BlockSpec((d, tf), lambda bi, si, fi: (0, fi)),
                  pl.BlockSpec((d, tf), lambda bi, si, fi: (0, nf + fi)),
                  pl.BlockSpec((1, CONV_WIDTH, tf), lambda bi, si, fi: (fi, 0, 0)),
                  pl.BlockSpec((1, CONV_WIDTH, tf), lambda bi, si, fi: (nf + fi, 0, 0)),
                  pl.BlockSpec((1, 1, tf), lambda bi, si, fi: (fi, 0, 0)),
                  pl.BlockSpec((1, 1, tf), lambda bi, si, fi: (nf + fi, 0, 0)),
                  pl.BlockSpec((tf, d), lambda bi, si, fi: (fi, 0)),
                  pl.BlockSpec((1, d), fix2),
                  pl.BlockSpec((1, 1, d), row)],
        out_specs=pl.BlockSpec((1, ts, d), tok),
        out_shape=jax.ShapeDtypeStruct((b, s, d), F32),
        scratch_shapes=[pltpu.VMEM((ts, d), BF16),
                        pltpu.VMEM((ts, d), F32),
                        pltpu.VMEM((2 * nf, CONV_HALO, tf), F32),
                        pltpu.VMEM((CONV_HALO + ts, tf), F32)],
        compiler_params=_params("arbitrary", "arbitrary", "arbitrary"),
        name="conv_ffn",
    )(x, sc, sh, g_pre, w_up, w_up, cw, cw, cb, cb, w_down, g_post, gt)


def _rotary_tables(s):
    half = ROT_DIM // 2
    pos = jnp.arange(s, dtype=F32)
    inv_freq = jnp.power(jnp.float32(ROPE_THETA), -jnp.arange(0, ROT_DIM, 2, dtype=F32) / ROT_DIM)
    ang = pos[:, None] * inv_freq[None, :]
    cos = jnp.cos(ang)
    sin = jnp.sin(ang)
    ones = jnp.ones((s, HEAD_DIM - ROT_DIM), F32)
    zeros = jnp.zeros((s, HEAD_DIM - ROT_DIM), F32)
    zh = jnp.zeros((s, half), F32)
    cos_h = jnp.concatenate([cos, cos, ones], axis=1)
    sa_h = jnp.concatenate([-sin, zh, zeros], axis=1)
    sb_h = jnp.concatenate([zh, sin, zeros], axis=1)
    rep = lambda t: jnp.concatenate([t] * HEADS_PER_GROUP, axis=1)
    return rep(cos_h), rep(sa_h), rep(sb_h)


def kernel(x, c, w_ada, b_ada, g_pre_mix, w_in, w_pool, pool_scale, attn_out_gain, pool_out_gain,
           w_out, g_post_mix, g_pre_ffn, w_up, conv_w, conv_b, w_down, g_post_ffn):
    b, s, d = x.shape
    depth = w_ada.shape[0]
    assert s % MOBA_BLOCK == 0 and d % LANES == 0
    mod = _mod_call(c, w_ada, b_ada)
    cos_t, sa_t, sb_t = _rotary_tables(s)
    for l in range(depth):
        sh1, sc1, gt1, sh2, sc2, gt2 = (mod[l, :, i * d:(i + 1) * d].reshape(b, 1, d)
                                        for i in range(6))
        q, k, vt, pn = _in_call(x, sc1, sh1, g_pre_mix[l][None], w_in[l].astype(BF16),
                                cos_t, sa_t, sb_t, w_pool[l].astype(BF16),
                                pool_scale[l][None], pool_out_gain[l][None])
        ao = _attn_call(q, k, vt)
        x = _out_call(ao, pn, x, attn_out_gain[l][None], w_out[l].astype(BF16),
                      g_post_mix[l][None], gt1)
        x = _ffn_call(x, sc2, sh2, g_pre_ffn[l][None], w_up[l].astype(BF16), conv_w[l], conv_b[l],
                      w_down[l].astype(BF16), g_post_ffn[l][None], gt2)
    return x
```

```python
import functools

import jax
import jax.numpy as jnp
from jax import lax
from jax.experimental import pallas as pl
from jax.experimental.pallas import tpu as pltpu

F32 = jnp.float32
BF16 = jnp.bfloat16

HEAD_DIM = 64
HEADS_PER_GROUP = 2
LANES = 128
MOBA_BLOCK = 256
MOBA_TOP_K = 3
POOL_WINDOWS = (2, 4, 8, 16)
POOL_HALO = 16
ROT_DIM = HEAD_DIM // 4
ROPE_THETA = 500000.0
CONV_WIDTH = 3
CONV_HALO = 8
NORM_EPS = 1e-6
NEG_INF = -1e30
LOG2_E = 1.4426950408889634
VT_ROWS = HEAD_DIM + 16
PAIR_KEYS = 2 * MOBA_BLOCK
VMEM_LIMIT = 48 * 1024 * 1024

_NT = (((1,), (1,)), ((), ()))


def _rms(x):
    return x * lax.rsqrt(jnp.mean(x * x, axis=-1, keepdims=True) + NORM_EPS)


def _params(*sem):
    return pltpu.CompilerParams(dimension_semantics=sem, vmem_limit_bytes=VMEM_LIMIT)


def _mod_kernel(c_ref, w_ref, b_ref, o_ref):
    c = c_ref[...]
    c_act = c * jax.nn.sigmoid(c)
    o_ref[0] = jnp.dot(c_act.astype(BF16), w_ref[0].astype(BF16),
                       preferred_element_type=F32) + b_ref[0]


def _mod_call(c, w_ada, b_ada, tn=1536):
    nl, d, n6 = w_ada.shape
    b = c.shape[0]
    return pl.pallas_call(
        _mod_kernel,
        grid=(nl, n6 // tn),
        in_specs=[pl.BlockSpec((b, d), lambda l, j: (0, 0)),
                  pl.BlockSpec((1, d, tn), lambda l, j: (l, 0, j)),
                  pl.BlockSpec((1, 1, tn), lambda l, j: (l, 0, j))],
        out_specs=pl.BlockSpec((1, b, tn), lambda l, j: (l, 0, j)),
        out_shape=jax.ShapeDtypeStruct((nl, b, n6), F32),
        compiler_params=_params("arbitrary", "arbitrary"),
        name="adaln_mod",
    )(c, w_ada, b_ada.reshape(nl, 1, n6))


def _in_kernel(x_ref, sc_ref, sh_ref, g_ref, w_ref, cos_ref, sa_ref, sb_ref, wp_ref, ps_ref,
               pg_ref, q_ref, k_ref, vt_ref, pn_ref, halo_ref, *, ts, aw):
    si = pl.program_id(1)
    h = _rms(x_ref[0]) * g_ref[...]
    h = h * (1.0 + sc_ref[0]) + sh_ref[0]
    z = jnp.dot(h.astype(BF16), w_ref[...], preferred_element_type=F32)

    cos = cos_ref[...]
    sa = sa_ref[...]
    sb = sb_ref[...]

    def rot(t):
        return t * cos + pltpu.roll(t, LANES - ROT_DIM // 2, 1) * sa + pltpu.roll(t, ROT_DIM // 2, 1) * sb

    scale = HEAD_DIM ** -0.5 * LOG2_E
    ngroups = aw // LANES
    ones = jnp.ones((VT_ROWS - HEAD_DIM, PAIR_KEYS), BF16)
    for gi in range(ngroups):
        lo = gi * LANES
        q_ref[0, :, lo:lo + LANES] = (rot(z[:, lo:lo + LANES]) * scale).astype(BF16)
        k_ref[0, :, lo:lo + LANES] = rot(z[:, aw + lo:aw + lo + LANES]).astype(BF16)
        vt = z[:, 2 * aw + lo:2 * aw + lo + LANES].T.astype(BF16)
        for hh in range(HEADS_PER_GROUP):
            head = gi * HEADS_PER_GROUP + hh
            for jp in range(ts // PAIR_KEYS):
                vt_ref[0, head, jp, 0:HEAD_DIM, :] = vt[hh * HEAD_DIM:(hh + 1) * HEAD_DIM,
                                                        jp * PAIR_KEYS:(jp + 1) * PAIR_KEYS]
                vt_ref[0, head, jp, HEAD_DIM:VT_ROWS, :] = ones

    pz = z[:, 3 * aw:]

    @pl.when(si == 0)
    def _():
        halo_ref[0:POOL_HALO, :] = jnp.zeros((POOL_HALO, pz.shape[1]), F32)

    halo_ref[POOL_HALO:POOL_HALO + ts, :] = pz
    tpos = si * ts + lax.broadcasted_iota(jnp.int32, (ts, 1), 0)
    outs = []
    for g, w in enumerate(POOL_WINDOWS):
        lo = g * LANES
        s = pz[:, lo:lo + LANES]
        for i in range(1, w):
            s = s + halo_ref[POOL_HALO - i:POOL_HALO - i + ts, lo:lo + LANES]
        cnt = jnp.minimum(tpos + 1, w).astype(F32)
        pooled = s / cnt - pz[:, lo:lo + LANES]
        outs.append(jnp.dot(pooled.astype(BF16), wp_ref[g], preferred_element_type=F32))
    halo_ref[0:POOL_HALO, :] = pz[ts - POOL_HALO:, :]
    po = jnp.concatenate(outs, axis=1) * ps_ref[...]
    pn_ref[0] = (_rms(po) * pg_ref[...]).astype(BF16)


def _in_call(x, sc, sh, g, w_in, cos_t, sa_t, sb_t, w_pool, pool_scale, pool_gain, ts=512):
    b, s, d = x.shape
    n_in = w_in.shape[1]
    pw = w_pool.shape[0] * w_pool.shape[1]
    aw = (n_in - pw) // 3
    nheads = aw // HEAD_DIM
    nb = s // MOBA_BLOCK
    tok = lambda bi, si: (bi, si, 0)
    row = lambda bi, si: (bi, 0, 0)
    fix2 = lambda bi, si: (0, 0)
    return pl.pallas_call(
        functools.partial(_in_kernel, ts=ts, aw=aw),
        grid=(b, s // ts),
        in_specs=[pl.BlockSpec((1, ts, d), tok),
                  pl.BlockSpec((1, 1, d), row),
                  pl.BlockSpec((1, 1, d), row),
                  pl.BlockSpec((1, d), fix2),
                  pl.BlockSpec((d, n_in), fix2),
                  pl.BlockSpec((ts, LANES), lambda bi, si: (si, 0)),
                  pl.BlockSpec((ts, LANES), lambda bi, si: (si, 0)),
                  pl.BlockSpec((ts, LANES), lambda bi, si: (si, 0)),
                  pl.BlockSpec(w_pool.shape, lambda bi, si: (0, 0, 0)),
                  pl.BlockSpec((1, pw), fix2),
                  pl.BlockSpec((1, pw), fix2)],
        out_specs=[pl.BlockSpec((1, ts, aw), tok),
                   pl.BlockSpec((1, ts, aw), tok),
                   pl.BlockSpec((1, nheads, ts // PAIR_KEYS, VT_ROWS, PAIR_KEYS),
                                lambda bi, si: (bi, 0, si, 0, 0)),
                   pl.BlockSpec((1, ts, pw), tok)],
        out_shape=[jax.ShapeDtypeStruct((b, s, aw), BF16),
                   jax.ShapeDtypeStruct((b, s, aw), BF16),
                   jax.ShapeDtypeStruct((b, nheads, s // PAIR_KEYS, VT_ROWS, PAIR_KEYS), BF16),
                   jax.ShapeDtypeStruct((b, s, pw), BF16)],
        scratch_shapes=[pltpu.VMEM((POOL_HALO + ts, pw), F32)],
        compiler_params=_params("arbitrary", "arbitrary"),
        name="in_proj",
    )(x, sc, sh, g, w_in, cos_t, sa_t, sb_t, w_pool, pool_scale, pool_gain)


def _attn_kernel(q_ref, k_ref, vt_ref, o_ref, km_ref, bias_ref, s0_ref, s1_ref, p0_ref, p1_ref,
                 *, nb):
    s_slots = (s0_ref, s1_ref)
    p_slots = (p0_ref, p1_ref)
    n = pl.program_id(2)
    blk = MOBA_BLOCK

    @pl.when(n == 0)
    def _():
        for j in range(nb):
            kj = k_ref[0, j * blk:(j + 1) * blk, :].astype(F32)
            km_ref[j:j + 1, :] = jnp.mean(kj, axis=0, keepdims=True)

    q2 = q_ref[0]
    lane = lax.broadcasted_iota(jnp.int32, (1, LANES), 1)
    km = km_ref[...]
    km_hi = km.astype(BF16)
    km_lo = (km - km_hi.astype(F32)).astype(BF16)
    jidx = lax.broadcasted_iota(jnp.int32, (nb, 1), 0)
    past = jidx < n
    heads = range(HEADS_PER_GROUP)

    qh = []
    for hh in heads:
        in_head = (lane >= hh * HEAD_DIM) & (lane < (hh + 1) * HEAD_DIM)
        qh.append(jnp.where(in_head, q2, jnp.zeros_like(q2)))

        gate = (lax.dot_general(km_hi, qh[hh], _NT, preferred_element_type=F32)
                + lax.dot_general(km_lo, qh[hh], _NT, preferred_element_type=F32))
        gate = jnp.where(past, gate, NEG_INF)
        rank = jnp.zeros(gate.shape, jnp.int32)
        for i in range(nb):
            gi = gate[i:i + 1, :]
            beats = (gi > gate) | ((gi == gate) & (i < jidx))
            rank = rank + beats.astype(jnp.int32)
        sel = (rank < MOBA_TOP_K) & past
        bias = jnp.where(sel, 0.0, NEG_INF).astype(F32)
        for j in range(nb):
            bias_ref[hh, j] = bias[j:j + 1, :]

    causal = (lax.broadcasted_iota(jnp.int32, (blk, blk), 0)
              <= lax.broadcasted_iota(jnp.int32, (blk, blk), 1))
    kd = k_ref[0, pl.ds(pl.multiple_of(n * blk, blk), blk), :]

    npairs = nb // 2
    trips = (n + 1) // 2

    def issue_scores(pi, slot):
        kp = k_ref[0, pl.ds(pl.multiple_of(pi * (2 * blk), 2 * blk), 2 * blk), :]
        for hh in heads:
            s_slots[slot][hh] = lax.dot_general(kp, qh[hh], _NT, preferred_element_type=F32)

    def weighted_values(hh, pi, slot):
        return jnp.dot(vt_ref[0, hh, pi], p_slots[slot][hh], preferred_element_type=F32)

    own_half = pl.multiple_of((n % 2) * blk, blk)
    other_half = pl.multiple_of((1 - n % 2) * blk, blk)
    init = []
    for hh in heads:
        s = lax.dot_general(kd, qh[hh], _NT, preferred_element_type=F32)
        s = jnp.where(causal, s, NEG_INF)
        m = jnp.max(s, axis=0, keepdims=True)
        p_slots[1][hh, pl.ds(own_half, blk), :] = jnp.exp2(s - m).astype(BF16)
        p_slots[1][hh, pl.ds(other_half, blk), :] = jnp.zeros((blk, blk), BF16)
        init.append((m, jnp.ones_like(m), jnp.zeros((VT_ROWS, blk), F32)))
    issue_scores(0, 0)

    def trip(i, carry, slot):
        issue_scores(jnp.minimum(i + 1, npairs - 1), 1 - slot)
        pending = jnp.where(i == 0, n // 2, i - 1)
        accs = [carry[hh][1] * carry[hh][2] + weighted_values(hh, pending, 1 - slot) for hh in heads]
        out = []
        for hh in heads:
            m = carry[hh][0]
            b0 = bias_ref[hh, 2 * i]
            b1 = bias_ref[hh, 2 * i + 1]
            s0 = s_slots[slot][hh, 0:blk, :]
            s1 = s_slots[slot][hh, blk:2 * blk, :]
            m_new = jnp.maximum(m, jnp.maximum(jnp.max(s0, axis=0, keepdims=True) + b0,
                                               jnp.max(s1, axis=0, keepdims=True) + b1))
            p_slots[slot][hh, 0:blk, :] = jnp.exp2(s0 - (m_new - b0)).astype(BF16)
            p_slots[slot][hh, blk:2 * blk, :] = jnp.exp2(s1 - (m_new - b1)).astype(BF16)
            out.append((m_new, jnp.exp2(m - m_new), accs[hh]))
        return tuple(out)

    def body(i, carry):
        return lax.cond(i % 2 == 0, lambda: trip(i, carry, 0), lambda: trip(i, carry, 1))

    carry = lax.fori_loop(0, trips, body, tuple(init))
    pending = jnp.where(trips == 0, n // 2, trips - 1)

    def drain(slot):
        return tuple(carry[hh][1] * carry[hh][2] + weighted_values(hh, pending, slot) for hh in heads)

    accs = lax.cond(trips % 2 == 1, lambda: drain(0), lambda: drain(1))
    outs = []
    for hh in heads:
        acc = accs[hh]
        outs.append(acc[0:HEAD_DIM, :] / acc[HEAD_DIM:HEAD_DIM + 1, :])
    o_ref[0] = jnp.concatenate(outs, axis=0).T


def _attn_call(q, k, vt):
    b, s, aw = q.shape
    ngroups = aw // LANES
    nb = s // MOBA_BLOCK
    return pl.pallas_call(
        functools.partial(_attn_kernel, nb=nb),
        grid=(b, ngroups, nb),
        in_specs=[pl.BlockSpec((1, MOBA_BLOCK, LANES), lambda bi, gi, n: (bi, n, gi)),
                  pl.BlockSpec((1, s, LANES), lambda bi, gi, n: (bi, 0, gi)),
                  pl.BlockSpec((1, HEADS_PER_GROUP, nb // 2, VT_ROWS, PAIR_KEYS),
                               lambda bi, gi, n: (bi, gi, 0, 0, 0))],
        out_specs=pl.BlockSpec((1, MOBA_BLOCK, LANES), lambda bi, gi, n: (bi, n, gi)),
        out_shape=jax.ShapeDtypeStruct((b, s, aw), F32),
        scratch_shapes=[pltpu.VMEM((nb, LANES), F32),
                        pltpu.VMEM((HEADS_PER_GROUP, nb, 1, MOBA_BLOCK), F32),
                        pltpu.VMEM((HEADS_PER_GROUP, PAIR_KEYS, MOBA_BLOCK), F32),
                        pltpu.VMEM((HEADS_PER_GROUP, PAIR_KEYS, MOBA_BLOCK), F32),
                        pltpu.VMEM((HEADS_PER_GROUP, PAIR_KEYS, MOBA_BLOCK), BF16),
                        pltpu.VMEM((HEADS_PER_GROUP, PAIR_KEYS, MOBA_BLOCK), BF16)],
        compiler_params=_params("arbitrary", "arbitrary", "arbitrary"),
        name="moba_attn",
    )(q, k, vt)


def _out_kernel(ao_ref, pn_ref, x_ref, ga_ref, w_ref, gp_ref, gt_ref, o_ref, *, aw):
    an = (_rms(ao_ref[0]) * ga_ref[...]).astype(BF16)
    y = (jnp.dot(an, w_ref[0:aw, :], preferred_element_type=F32)
         + jnp.dot(pn_ref[0], w_ref[aw:, :], preferred_element_type=F32))
    o_ref[0] = x_ref[0] + gt_ref[0] * (_rms(y) * gp_ref[...])


def _out_call(ao, pn, x, attn_gain, w_out, g_post, gt, ts=512):
    b, s, d = x.shape
    aw = ao.shape[2]
    pw = pn.shape[2]
    tok = lambda bi, si: (bi, si, 0)
    fix2 = lambda bi, si: (0, 0)
    return pl.pallas_call(
        functools.partial(_out_kernel, aw=aw),
        grid=(b, s // ts),
        in_specs=[pl.BlockSpec((1, ts, aw), tok),
                  pl.BlockSpec((1, ts, pw), tok),
                  pl.BlockSpec((1, ts, d), tok),
                  pl.BlockSpec((1, aw), fix2),
                  pl.BlockSpec(w_out.shape, fix2),
                  pl.BlockSpec((1, d), fix2),
                  pl.BlockSpec((1, 1, d), lambda bi, si: (bi, 0, 0))],
        out_specs=pl.BlockSpec((1, ts, d), tok),
        out_shape=jax.ShapeDtypeStruct((b, s, d), F32),
        compiler_params=_params("arbitrary", "arbitrary"),
        name="out_proj",
    )(ao, pn, x, attn_gain, w_out, g_post, gt)


def _ffn_kernel(x_ref, sc_ref, sh_ref, g_ref, wu_ref, cw_ref, cb_ref, wd_ref, gp_ref, gt_ref,
                o_ref, tail_ref, act_ref, *, ts, tf):
    si = pl.program_id(1)
    x = x_ref[0]
    hb = ((_rms(x) * g_ref[...]) * (1.0 + sc_ref[0]) + sh_ref[0]).astype(BF16)
    d_ff = wd_ref.shape[0]
    first = si == 0

    def conv(lo):
        u = jnp.dot(hb, wu_ref[:, lo:lo + tf], preferred_element_type=F32)
        tail = jnp.where(first, 0.0, tail_ref[:, lo:lo + tf])
        tail_ref[:, lo:lo + tf] = u[ts - CONV_HALO:, :]
        ucat = jnp.concatenate([tail, u], axis=0)
        uc = cb_ref[:, lo:lo + tf] + cw_ref[CONV_WIDTH - 1:CONV_WIDTH, lo:lo + tf] * u
        for j in range(CONV_WIDTH - 1):
            back = CONV_WIDTH - 1 - j
            uc = uc + cw_ref[j:j + 1, lo:lo + tf] * ucat[CONV_HALO - back:CONV_HALO - back + ts, :]
        return uc

    for fi in range(d_ff // tf):
        a = conv(fi * tf)
        g = conv(d_ff + fi * tf)
        act_ref[:, fi * tf:(fi + 1) * tf] = (a * jax.nn.sigmoid(a) * g).astype(BF16)
    y = jnp.dot(act_ref[...], wd_ref[...], preferred_element_type=F32)
    o_ref[0] = x + gt_ref[0] * (_rms(y) * gp_ref[...])


def _ffn_call(x, sc, sh, g_pre, w_up, conv_w, conv_b, w_down, g_post, gt, ts=512, tf=256):
    b, s, d = x.shape
    f = w_down.shape[0]
    tok = lambda bi, si: (bi, si, 0)
    row = lambda bi, si: (bi, 0, 0)
    fix2 = lambda bi, si: (0, 0)
    resident = dict(pipeline_mode=pl.Buffered(1))
    return pl.pallas_call(
        functools.partial(_ffn_kernel, ts=ts, tf=tf),
        grid=(b, s // ts),
        in_specs=[pl.BlockSpec((1, ts, d), tok),
                  pl.BlockSpec((1, 1, d), row),
                  pl.BlockSpec((1, 1, d), row),
                  pl.BlockSpec((1, d), fix2),
                  pl.BlockSpec(w_up.shape, fix2, **resident),
                  pl.BlockSpec(conv_w.shape, fix2),
                  pl.BlockSpec((1, 2 * f), fix2),
                  pl.BlockSpec(w_down.shape, fix2, **resident),
                  pl.BlockSpec((1, d), fix2),
                  pl.BlockSpec((1, 1, d), row)],
        out_specs=pl.BlockSpec((1, ts, d), tok),
        out_shape=jax.ShapeDtypeStruct((b, s, d), F32),
        scratch_shapes=[pltpu.VMEM((CONV_HALO, 2 * f), F32),
                        pltpu.VMEM((ts, f), BF16)],
        compiler_params=_params("arbitrary", "arbitrary"),
        name="conv_ffn",
    )(x, sc, sh, g_pre, w_up, conv_w, conv_b[None], w_down, g_post, gt)


def _rotary_tables(s):
    half = ROT_DIM // 2
    pos = jnp.arange(s, dtype=F32)
    inv_freq = jnp.power(jnp.float32(ROPE_THETA), -jnp.arange(0, ROT_DIM, 2, dtype=F32) / ROT_DIM)
    ang = pos[:, None] * inv_freq[None, :]
    cos = jnp.cos(ang)
    sin = jnp.sin(ang)
    ones = jnp.ones((s, HEAD_DIM - ROT_DIM), F32)
    zeros = jnp.zeros((s, HEAD_DIM - ROT_DIM), F32)
    zh = jnp.zeros((s, half), F32)
    cos_h = jnp.concatenate([cos, cos, ones], axis=1)
    sa_h = jnp.concatenate([-sin, zh, zeros], axis=1)
    sb_h = jnp.concatenate([zh, sin, zeros], axis=1)
    rep = lambda t: jnp.concatenate([t] * HEADS_PER_GROUP, axis=1)
    return rep(cos_h), rep(sa_h), rep(sb_h)


def kernel(x, c, w_ada, b_ada, g_pre_mix, w_in, w_pool, pool_scale, attn_out_gain, pool_out_gain,
           w_out, g_post_mix, g_pre_ffn, w_up, conv_w, conv_b, w_down, g_post_ffn):
    b, s, d = x.shape
    depth = w_ada.shape[0]
    assert s % MOBA_BLOCK == 0 and d % LANES == 0
    mod = _mod_call(c, w_ada, b_ada)
    cos_t, sa_t, sb_t = _rotary_tables(s)
    for l in range(depth):
        sh1, sc1, gt1, sh2, sc2, gt2 = (mod[l, :, i * d:(i + 1) * d].reshape(b, 1, d)
                                        for i in range(6))
        q, k, vt, pn = _in_call(x, sc1, sh1, g_pre_mix[l][None], w_in[l].astype(BF16),
                                cos_t, sa_t, sb_t, w_pool[l].astype(BF16),
                                pool_scale[l][None], pool_out_gain[l][None])
        ao = _attn_call(q, k, vt)
        x = _out_call(ao, pn, x, attn_out_gain[l][None], w_out[l].astype(BF16),
                      g_post_mix[l][None], gt1)
        x = _ffn_call(x, sc2, sh2, g_pre_ffn[l][None], w_up[l].astype(BF16), conv_w[l], conv_b[l],
                      w_down[l].astype(BF16), g_post_ffn[l][None], gt2)
    return x
```

```python
import functools

import jax
import jax.numpy as jnp
from jax import lax
from jax.experimental import pallas as pl
from jax.experimental.pallas import tpu as pltpu

F32 = jnp.float32
BF16 = jnp.bfloat16

HEAD_DIM = 64
HEADS_PER_GROUP = 2
LANES = 128
MOBA_BLOCK = 256
MOBA_TOP_K = 3
POOL_WINDOWS = (2, 4, 8, 16)
POOL_HALO = 16
ROT_DIM = HEAD_DIM // 4
ROPE_THETA = 500000.0
CONV_WIDTH = 3
CONV_HALO = 8
NORM_EPS = 1e-6
NEG_INF = -1e30
LOG2_E = 1.4426950408889634
VT_ROWS = HEAD_DIM + 16
PAIR_KEYS = 2 * MOBA_BLOCK
VMEM_LIMIT = 48 * 1024 * 1024

_NT = (((1,), (1,)), ((), ()))


def _rms(x):
    return x * lax.rsqrt(jnp.mean(x * x, axis=-1, keepdims=True) + NORM_EPS)


def _params(*sem):
    return pltpu.CompilerParams(dimension_semantics=sem, vmem_limit_bytes=VMEM_LIMIT)


def _mod_kernel(c_ref, w_ref, b_ref, o_ref):
    c = c_ref[...]
    c_act = c * jax.nn.sigmoid(c)
    o_ref[0] = jnp.dot(c_act.astype(BF16), w_ref[0].astype(BF16),
                       preferred_element_type=F32) + b_ref[0]


def _mod_call(c, w_ada, b_ada, tn=1536):
    nl, d, n6 = w_ada.shape
    b = c.shape[0]
    return pl.pallas_call(
        _mod_kernel,
        grid=(nl, n6 // tn),
        in_specs=[pl.BlockSpec((b, d), lambda l, j: (0, 0)),
                  pl.BlockSpec((1, d, tn), lambda l, j: (l, 0, j)),
                  pl.BlockSpec((1, 1, tn), lambda l, j: (l, 0, j))],
        out_specs=pl.BlockSpec((1, b, tn), lambda l, j: (l, 0, j)),
        out_shape=jax.ShapeDtypeStruct((nl, b, n6), F32),
        compiler_params=_params("arbitrary", "arbitrary"),
        name="adaln_mod",
    )(c, w_ada, b_ada.reshape(nl, 1, n6))


def _in_kernel(x_ref, sc_ref, sh_ref, g_ref, w_ref, cos_ref, sa_ref, sb_ref, wp_ref, ps_ref,
               pg_ref, q_ref, k_ref, vt_ref, pn_ref, halo_ref, *, ts, aw):
    si = pl.program_id(1)
    h = _rms(x_ref[0]) * g_ref[...]
    h = h * (1.0 + sc_ref[0]) + sh_ref[0]
    z = jnp.dot(h.astype(BF16), w_ref[...], preferred_element_type=F32)

    cos = cos_ref[...]
    sa = sa_ref[...]
    sb = sb_ref[...]

    def rot(t):
        return t * cos + pltpu.roll(t, LANES - ROT_DIM // 2, 1) * sa + pltpu.roll(t, ROT_DIM // 2, 1) * sb

    scale = HEAD_DIM ** -0.5 * LOG2_E
    ngroups = aw // LANES
    ones = jnp.ones((VT_ROWS - HEAD_DIM, PAIR_KEYS), BF16)
    for gi in range(ngroups):
        lo = gi * LANES
        q_ref[0, :, lo:lo + LANES] = (rot(z[:, lo:lo + LANES]) * scale).astype(BF16)
        k_ref[0, :, lo:lo + LANES] = rot(z[:, aw + lo:aw + lo + LANES]).astype(BF16)
        vt = z[:, 2 * aw + lo:2 * aw + lo + LANES].T.astype(BF16)
        for hh in range(HEADS_PER_GROUP):
            head = gi * HEADS_PER_GROUP + hh
            for jp in range(ts // PAIR_KEYS):
                vt_ref[0, head, jp, 0:HEAD_DIM, :] = vt[hh * HEAD_DIM:(hh + 1) * HEAD_DIM,
                                                        jp * PAIR_KEYS:(jp + 1) * PAIR_KEYS]
                vt_ref[0, head, jp, HEAD_DIM:VT_ROWS, :] = ones

    pz = z[:, 3 * aw:]

    @pl.when(si == 0)
    def _():
        halo_ref[0:POOL_HALO, :] = jnp.zeros((POOL_HALO, pz.shape[1]), F32)

    halo_ref[POOL_HALO:POOL_HALO + ts, :] = pz
    tpos = si * ts + lax.broadcasted_iota(jnp.int32, (ts, 1), 0)
    outs = []
    for g, w in enumerate(POOL_WINDOWS):
        lo = g * LANES
        s = pz[:, lo:lo + LANES]
        for i in range(1, w):
            s = s + halo_ref[POOL_HALO - i:POOL_HALO - i + ts, lo:lo + LANES]
        cnt = jnp.minimum(tpos + 1, w).astype(F32)
        pooled = s / cnt - pz[:, lo:lo + LANES]
        outs.append(jnp.dot(pooled.astype(BF16), wp_ref[g], preferred_element_type=F32))
    halo_ref[0:POOL_HALO, :] = pz[ts - POOL_HALO:, :]
    po = jnp.concatenate(outs, axis=1) * ps_ref[...]
    pn_ref[0] = (_rms(po) * pg_ref[...]).astype(BF16)


def _in_call(x, sc, sh, g, w_in, cos_t, sa_t, sb_t, w_pool, pool_scale, pool_gain, ts=512):
    b, s, d = x.shape
    n_in = w_in.shape[1]
    pw = w_pool.shape[0] * w_pool.shape[1]
    aw = (n_in - pw) // 3
    nheads = aw // HEAD_DIM
    tok = lambda bi, si: (bi, si, 0)
    row = lambda bi, si: (bi, 0, 0)
    fix2 = lambda bi, si: (0, 0)
    return pl.pallas_call(
        functools.partial(_in_kernel, ts=ts, aw=aw),
        grid=(b, s // ts),
        in_specs=[pl.BlockSpec((1, ts, d), tok),
                  pl.BlockSpec((1, 1, d), row),
                  pl.BlockSpec((1, 1, d), row),
                  pl.BlockSpec((1, d), fix2),
                  pl.BlockSpec((d, n_in), fix2),
                  pl.BlockSpec((ts, LANES), lambda bi, si: (si, 0)),
                  pl.BlockSpec((ts, LANES), lambda bi, si: (si, 0)),
                  pl.BlockSpec((ts, LANES), lambda bi, si: (si, 0)),
                  pl.BlockSpec(w_pool.shape, lambda bi, si: (0, 0, 0)),
                  pl.BlockSpec((1, pw), fix2),
                  pl.BlockSpec((1, pw), fix2)],
        out_specs=[pl.BlockSpec((1, ts, aw), tok),
                   pl.BlockSpec((1, ts, aw), tok),
                   pl.BlockSpec((1, nheads, ts // PAIR_KEYS, VT_ROWS, PAIR_KEYS),
                                lambda bi, si: (bi, 0, si, 0, 0)),
                   pl.BlockSpec((1, ts, pw), tok)],
        out_shape=[jax.ShapeDtypeStruct((b, s, aw), BF16),
                   jax.ShapeDtypeStruct((b, s, aw), BF16),
                   jax.ShapeDtypeStruct((b, nheads, s // PAIR_KEYS, VT_ROWS, PAIR_KEYS), BF16),
                   jax.ShapeDtypeStruct((b, s, pw), BF16)],
        scratch_shapes=[pltpu.VMEM((POOL_HALO + ts, pw), F32)],
        compiler_params=_params("arbitrary", "arbitrary"),
        name="in_proj",
    )(x, sc, sh, g, w_in, cos_t, sa_t, sb_t, w_pool, pool_scale, pool_gain)


def _attn_kernel(q_ref, k_ref, vt_ref, o_ref, km_ref, bias_ref, s0_ref, s1_ref, *, nb):
    s_slots = (s0_ref, s1_ref)
    n = pl.program_id(2)
    blk = MOBA_BLOCK
    heads = range(HEADS_PER_GROUP)

    @pl.when(n == 0)
    def _():
        for j in range(nb):
            kj = k_ref[0, j * blk:(j + 1) * blk, :].astype(F32)
            km_ref[j:j + 1, :] = jnp.mean(kj, axis=0, keepdims=True)

    q2 = q_ref[0]
    lane = lax.broadcasted_iota(jnp.int32, (1, LANES), 1)
    qh = [jnp.where((lane >= hh * HEAD_DIM) & (lane < (hh + 1) * HEAD_DIM), q2, jnp.zeros_like(q2))
          for hh in heads]

    npairs = nb // 2
    trips = (n + 1) // 2

    def issue_scores(pi, slot):
        kp = k_ref[0, pl.ds(pl.multiple_of(pi * PAIR_KEYS, PAIR_KEYS), PAIR_KEYS), :]
        for hh in heads:
            s_slots[slot][hh] = lax.dot_general(kp, qh[hh], _NT, preferred_element_type=F32)

    issue_scores(0, 0)
    kd = k_ref[0, pl.ds(pl.multiple_of(n * blk, blk), blk), :]
    s_own = [lax.dot_general(kd, qh[hh], _NT, preferred_element_type=F32) for hh in heads]

    km = km_ref[...]
    km_hi = km.astype(BF16)
    km_lo = (km - km_hi.astype(F32)).astype(BF16)
    jidx = lax.broadcasted_iota(jnp.int32, (nb, 1), 0)
    past = jidx < n
    for hh in heads:
        gate = (lax.dot_general(km_hi, qh[hh], _NT, preferred_element_type=F32)
                + lax.dot_general(km_lo, qh[hh], _NT, preferred_element_type=F32))
        gate = jnp.where(past, gate, NEG_INF)
        rank = jnp.zeros(gate.shape, jnp.int32)
        for i in range(nb):
            gi = gate[i:i + 1, :]
            beats = (gi > gate) | ((gi == gate) & (i < jidx))
            rank = rank + beats.astype(jnp.int32)
        sel = (rank < MOBA_TOP_K) & past
        bias = jnp.where(sel, 0.0, NEG_INF).astype(F32)
        for j in range(nb):
            bias_ref[hh, j] = bias[j:j + 1, :]

    causal = (lax.broadcasted_iota(jnp.int32, (blk, blk), 0)
              <= lax.broadcasted_iota(jnp.int32, (blk, blk), 1))
    own_half = pl.ds(pl.multiple_of((n % 2) * blk, blk), blk)
    init = []
    for hh in heads:
        s = jnp.where(causal, s_own[hh], NEG_INF)
        m = jnp.max(s, axis=0, keepdims=True)
        p = jnp.exp2(s - m).astype(BF16)
        init.append((m, jnp.dot(vt_ref[0, hh, n // 2, :, own_half], p, preferred_element_type=F32)))

    def trip(i, carry, slot):
        issue_scores(jnp.minimum(i + 1, npairs - 1), 1 - slot)
        out = []
        for hh in heads:
            m, acc = carry[hh]
            b0 = bias_ref[hh, 2 * i]
            b1 = bias_ref[hh, 2 * i + 1]
            s0 = s_slots[slot][hh, 0:blk, :]
            s1 = s_slots[slot][hh, blk:2 * blk, :]
            m_new = jnp.maximum(m, jnp.maximum(jnp.max(s0, axis=0, keepdims=True) + b0,
                                               jnp.max(s1, axis=0, keepdims=True) + b1))
            p = jnp.concatenate([jnp.exp2(s0 - (m_new - b0)).astype(BF16),
                                 jnp.exp2(s1 - (m_new - b1)).astype(BF16)], axis=0)
            acc = jnp.exp2(m - m_new) * acc + jnp.dot(vt_ref[0, hh, i], p, preferred_element_type=F32)
            out.append((m_new, acc))
        return tuple(out)

    def body(i, carry):
        return lax.cond(i % 2 == 0, lambda: trip(i, carry, 0), lambda: trip(i, carry, 1))

    carry = lax.fori_loop(0, trips, body, tuple(init))
    outs = []
    for hh in heads:
        acc = carry[hh][1]
        outs.append(acc[0:HEAD_DIM, :] / acc[HEAD_DIM:HEAD_DIM + 1, :])
    o_ref[0] = jnp.concatenate(outs, axis=0).T


def _attn_call(q, k, vt):
    b, s, aw = q.shape
    ngroups = aw // LANES
    nb = s // MOBA_BLOCK
    return pl.pallas_call(
        functools.partial(_attn_kernel, nb=nb),
        grid=(b, ngroups, nb),
        in_specs=[pl.BlockSpec((1, MOBA_BLOCK, LANES), lambda bi, gi, n: (bi, n, gi)),
                  pl.BlockSpec((1, s, LANES), lambda bi, gi, n: (bi, 0, gi)),
                  pl.BlockSpec((1, HEADS_PER_GROUP, nb // 2, VT_ROWS, PAIR_KEYS),
                               lambda bi, gi, n: (bi, gi, 0, 0, 0))],
        out_specs=pl.BlockSpec((1, MOBA_BLOCK, LANES), lambda bi, gi, n: (bi, n, gi)),
        out_shape=jax.ShapeDtypeStruct((b, s, aw), F32),
        scratch_shapes=[pltpu.VMEM((nb, LANES), F32),
                        pltpu.VMEM((HEADS_PER_GROUP, nb, 1, MOBA_BLOCK), F32),
                        pltpu.VMEM((HEADS_PER_GROUP, PAIR_KEYS, MOBA_BLOCK), F32),
                        pltpu.VMEM((HEADS_PER_GROUP, PAIR_KEYS, MOBA_BLOCK), F32)],
        compiler_params=_params("arbitrary", "arbitrary", "arbitrary"),
        name="moba_attn",
    )(q, k, vt)


def _out_kernel(ao_ref, pn_ref, x_ref, ga_ref, w_ref, gp_ref, gt_ref, o_ref, *, aw):
    an = (_rms(ao_ref[0]) * ga_ref[...]).astype(BF16)
    y = (jnp.dot(an, w_ref[0:aw, :], preferred_element_type=F32)
         + jnp.dot(pn_ref[0], w_ref[aw:, :], preferred_element_type=F32))
    o_ref[0] = x_ref[0] + gt_ref[0] * (_rms(y) * gp_ref[...])


def _out_call(ao, pn, x, attn_gain, w_out, g_post, gt, ts=512):
    b, s, d = x.shape
    aw = ao.shape[2]
    pw = pn.shape[2]
    tok = lambda bi, si: (bi, si, 0)
    fix2 = lambda bi, si: (0, 0)
    return pl.pallas_call(
        functools.partial(_out_kernel, aw=aw),
        grid=(b, s // ts),
        in_specs=[pl.BlockSpec((1, ts, aw), tok),
                  pl.BlockSpec((1, ts, pw), tok),
                  pl.BlockSpec((1, ts, d), tok),
                  pl.BlockSpec((1, aw), fix2),
                  pl.BlockSpec(w_out.shape, fix2),
                  pl.BlockSpec((1, d), fix2),
                  pl.BlockSpec((1, 1, d), lambda bi, si: (bi, 0, 0))],
        out_specs=pl.BlockSpec((1, ts, d), tok),
        out_shape=jax.ShapeDtypeStruct((b, s, d), F32),
        compiler_params=_params("arbitrary", "arbitrary"),
        name="out_proj",
    )(ao, pn, x, attn_gain, w_out, g_post, gt)


def _ffn_kernel(x_ref, sc_ref, sh_ref, g_ref, wu_ref, cw_ref, cb_ref, wd_ref, gp_ref, gt_ref,
                o_ref, tail_ref, act_ref, *, ts, tf):
    si = pl.program_id(1)
    x = x_ref[0]
    hb = ((_rms(x) * g_ref[...]) * (1.0 + sc_ref[0]) + sh_ref[0]).astype(BF16)
    d_ff = wd_ref.shape[0]
    first = si == 0

    def conv(lo):
        u = jnp.dot(hb, wu_ref[:, lo:lo + tf], preferred_element_type=F32)
        tail = jnp.where(first, 0.0, tail_ref[:, lo:lo + tf])
        tail_ref[:, lo:lo + tf] = u[ts - CONV_HALO:, :]
        ucat = jnp.concatenate([tail, u], axis=0)
        uc = cb_ref[:, lo:lo + tf] + cw_ref[CONV_WIDTH - 1:CONV_WIDTH, lo:lo + tf] * u
        for j in range(CONV_WIDTH - 1):
            back = CONV_WIDTH - 1 - j
            uc = uc + cw_ref[j:j + 1, lo:lo + tf] * ucat[CONV_HALO - back:CONV_HALO - back + ts, :]
        return uc

    for fi in range(d_ff // tf):
        a = conv(fi * tf)
        g = conv(d_ff + fi * tf)
        act_ref[:, fi * tf:(fi + 1) * tf] = (a * jax.nn.sigmoid(a) * g).astype(BF16)
    y = jnp.dot(act_ref[...], wd_ref[...], preferred_element_type=F32)
    o_ref[0] = x + gt_ref[0] * (_rms(y) * gp_ref[...])


def _ffn_call(x, sc, sh, g_pre, w_up, conv_w, conv_b, w_down, g_post, gt, ts=512, tf=256):
    b, s, d = x.shape
    f = w_down.shape[0]
    tok = lambda bi, si: (bi, si, 0)
    row = lambda bi, si: (bi, 0, 0)
    fix2 = lambda bi, si: (0, 0)
    resident = dict(pipeline_mode=pl.Buffered(1))
    return pl.pallas_call(
        functools.partial(_ffn_kernel, ts=ts, tf=tf),
        grid=(b, s // ts),
        in_specs=[pl.BlockSpec((1, ts, d), tok),
                  pl.BlockSpec((1, 1, d), row),
                  pl.BlockSpec((1, 1, d), row),
                  pl.BlockSpec((1, d), fix2),
                  pl.BlockSpec(w_up.shape, fix2, **resident),
                  pl.BlockSpec(conv_w.shape, fix2),
                  pl.BlockSpec((1, 2 * f), fix2),
                  pl.BlockSpec(w_down.shape, fix2, **resident),
                  pl.BlockSpec((1, d), fix2),
                  pl.BlockSpec((1, 1, d), row)],
        out_specs=pl.BlockSpec((1, ts, d), tok),
        out_shape=jax.ShapeDtypeStruct((b, s, d), F32),
        scratch_shapes=[pltpu.VMEM((CONV_HALO, 2 * f), F32),
                        pltpu.VMEM((ts, f), BF16)],
        compiler_params=_params("arbitrary", "arbitrary"),
        name="conv_ffn",
    )(x, sc, sh, g_pre, w_up, conv_w, conv_b[None], w_down, g_post, gt)


def _rotary_tables(s):
    half = ROT_DIM // 2
    pos = jnp.arange(s, dtype=F32)
    inv_freq = jnp.power(jnp.float32(ROPE_THETA), -jnp.arange(0, ROT_DIM, 2, dtype=F32) / ROT_DIM)
    ang = pos[:, None] * inv_freq[None, :]
    cos = jnp.cos(ang)
    sin = jnp.sin(ang)
    ones = jnp.ones((s, HEAD_DIM - ROT_DIM), F32)
    zeros = jnp.zeros((s, HEAD_DIM - ROT_DIM), F32)
    zh = jnp.zeros((s, half), F32)
    cos_h = jnp.concatenate([cos, cos, ones], axis=1)
    sa_h = jnp.concatenate([-sin, zh, zeros], axis=1)
    sb_h = jnp.concatenate([zh, sin, zeros], axis=1)
    rep = lambda t: jnp.concatenate([t] * HEADS_PER_GROUP, axis=1)
    return rep(cos_h), rep(sa_h), rep(sb_h)


def kernel(x, c, w_ada, b_ada, g_pre_mix, w_in, w_pool, pool_scale, attn_out_gain, pool_out_gain,
           w_out, g_post_mix, g_pre_ffn, w_up, conv_w, conv_b, w_down, g_post_ffn):
    b, s, d = x.shape
    depth = w_ada.shape[0]
    assert s % PAIR_KEYS == 0 and d % LANES == 0
    mod = _mod_call(c, w_ada, b_ada)
    cos_t, sa_t, sb_t = _rotary_tables(s)
    for l in range(depth):
        sh1, sc1, gt1, sh2, sc2, gt2 = (mod[l, :, i * d:(i + 1) * d].reshape(b, 1, d)
                                        for i in range(6))
        q, k, vt, pn = _in_call(x, sc1, sh1, g_pre_mix[l][None], w_in[l].astype(BF16),
                                cos_t, sa_t, sb_t, w_pool[l].astype(BF16),
                                pool_scale[l][None], pool_out_gain[l][None])
        ao = _attn_call(q, k, vt)
        x = _out_call(ao, pn, x, attn_out_gain[l][None], w_out[l].astype(BF16),
                      g_post_mix[l][None], gt1)
        x = _ffn_call(x, sc2, sh2, g_pre_ffn[l][None], w_up[l].astype(BF16), conv_w[l], conv_b[l],
                      w_down[l].astype(BF16), g_post_ffn[l][None], gt2)
    return x
```

```python
import functools

import jax
import jax.numpy as jnp
from jax import lax
from jax.experimental import pallas as pl
from jax.experimental.pallas import tpu as pltpu

F32 = jnp.float32
BF16 = jnp.bfloat16

HEAD_DIM = 64
HEADS_PER_GROUP = 2
LANES = 128
MOBA_BLOCK = 256
MOBA_TOP_K = 3
POOL_WINDOWS = (2, 4, 8, 16)
POOL_HALO = 16
ROT_DIM = HEAD_DIM // 4
ROPE_THETA = 500000.0
CONV_WIDTH = 3
CONV_HALO = 8
NORM_EPS = 1e-6
NEG_INF = -1e30
LOG2_E = 1.4426950408889634
VT_ROWS = HEAD_DIM + 16
PAIR_KEYS = 2 * MOBA_BLOCK
VMEM_LIMIT = 48 * 1024 * 1024

_NT = (((1,), (1,)), ((), ()))


def _rms(x):
    return x * lax.rsqrt(jnp.mean(x * x, axis=-1, keepdims=True) + NORM_EPS)


def _params(*sem):
    return pltpu.CompilerParams(dimension_semantics=sem, vmem_limit_bytes=VMEM_LIMIT)


def _mod_kernel(c_ref, w_ref, b_ref, o_ref):
    c = c_ref[...]
    c_act = c * jax.nn.sigmoid(c)
    o_ref[0] = jnp.dot(c_act.astype(BF16), w_ref[0].astype(BF16),
                       preferred_element_type=F32) + b_ref[0]


def _mod_call(c, w_ada, b_ada, tn=1536):
    nl, d, n6 = w_ada.shape
    b = c.shape[0]
    return pl.pallas_call(
        _mod_kernel,
        grid=(nl, n6 // tn),
        in_specs=[pl.BlockSpec((b, d), lambda l, j: (0, 0)),
                  pl.BlockSpec((1, d, tn), lambda l, j: (l, 0, j)),
                  pl.BlockSpec((1, 1, tn), lambda l, j: (l, 0, j))],
        out_specs=pl.BlockSpec((1, b, tn), lambda l, j: (l, 0, j)),
        out_shape=jax.ShapeDtypeStruct((nl, b, n6), F32),
        compiler_params=_params("arbitrary", "arbitrary"),
        name="adaln_mod",
    )(c, w_ada, b_ada.reshape(nl, 1, n6))


def _in_kernel(x_ref, sc_ref, sh_ref, g_ref, w_ref, cos_ref, sa_ref, sb_ref, wp_ref, ps_ref,
               pg_ref, q_ref, k_ref, vt_ref, pn_ref, halo_ref, *, ts, aw):
    si = pl.program_id(1)
    h = _rms(x_ref[0]) * g_ref[...]
    h = h * (1.0 + sc_ref[0]) + sh_ref[0]
    z = jnp.dot(h.astype(BF16), w_ref[...], preferred_element_type=F32)

    cos = cos_ref[...]
    sa = sa_ref[...]
    sb = sb_ref[...]

    def rot(t):
        return t * cos + pltpu.roll(t, LANES - ROT_DIM // 2, 1) * sa + pltpu.roll(t, ROT_DIM // 2, 1) * sb

    scale = HEAD_DIM ** -0.5 * LOG2_E
    ngroups = aw // LANES
    ones = jnp.ones((VT_ROWS - HEAD_DIM, PAIR_KEYS), BF16)
    for gi in range(ngroups):
        lo = gi * LANES
        q_ref[0, :, lo:lo + LANES] = (rot(z[:, lo:lo + LANES]) * scale).astype(BF16)
        k_ref[0, :, lo:lo + LANES] = rot(z[:, aw + lo:aw + lo + LANES]).astype(BF16)
        vt = z[:, 2 * aw + lo:2 * aw + lo + LANES].T.astype(BF16)
        for hh in range(HEADS_PER_GROUP):
            head = gi * HEADS_PER_GROUP + hh
            for jp in range(ts // PAIR_KEYS):
                vt_ref[0, head, jp, 0:HEAD_DIM, :] = vt[hh * HEAD_DIM:(hh + 1) * HEAD_DIM,
                                                        jp * PAIR_KEYS:(jp + 1) * PAIR_KEYS]
                vt_ref[0, head, jp, HEAD_DIM:VT_ROWS, :] = ones

    pz = z[:, 3 * aw:]

    @pl.when(si == 0)
    def _():
        halo_ref[0:POOL_HALO, :] = jnp.zeros((POOL_HALO, pz.shape[1]), F32)

    halo_ref[POOL_HALO:POOL_HALO + ts, :] = pz
    tpos = si * ts + lax.broadcasted_iota(jnp.int32, (ts, 1), 0)
    outs = []
    for g, w in enumerate(POOL_WINDOWS):
        lo = g * LANES
        s = pz[:, lo:lo + LANES]
        for i in range(1, w):
            s = s + halo_ref[POOL_HALO - i:POOL_HALO - i + ts, lo:lo + LANES]
        cnt = jnp.minimum(tpos + 1, w).astype(F32)
        pooled = s / cnt - pz[:, lo:lo + LANES]
        outs.append(jnp.dot(pooled.astype(BF16), wp_ref[g], preferred_element_type=F32))
    halo_ref[0:POOL_HALO, :] = pz[ts - POOL_HALO:, :]
    po = jnp.concatenate(outs, axis=1) * ps_ref[...]
    pn_ref[0] = (_rms(po) * pg_ref[...]).astype(BF16)


def _in_call(x, sc, sh, g, w_in, cos_t, sa_t, sb_t, w_pool, pool_scale, pool_gain, ts=512):
    b, s, d = x.shape
    n_in = w_in.shape[1]
    pw = w_pool.shape[0] * w_pool.shape[1]
    aw = (n_in - pw) // 3
    nheads = aw // HEAD_DIM
    tok = lambda bi, si: (bi, si, 0)
    row = lambda bi, si: (bi, 0, 0)
    fix2 = lambda bi, si: (0, 0)
    return pl.pallas_call(
        functools.partial(_in_kernel, ts=ts, aw=aw),
        grid=(b, s // ts),
        in_specs=[pl.BlockSpec((1, ts, d), tok),
                  pl.BlockSpec((1, 1, d), row),
                  pl.BlockSpec((1, 1, d), row),
                  pl.BlockSpec((1, d), fix2),
                  pl.BlockSpec((d, n_in), fix2),
                  pl.BlockSpec((ts, LANES), lambda bi, si: (si, 0)),
                  pl.BlockSpec((ts, LANES), lambda bi, si: (si, 0)),
                  pl.BlockSpec((ts, LANES), lambda bi, si: (si, 0)),
                  pl.BlockSpec(w_pool.shape, lambda bi, si: (0, 0, 0)),
                  pl.BlockSpec((1, pw), fix2),
                  pl.BlockSpec((1, pw), fix2)],
        out_specs=[pl.BlockSpec((1, ts, aw), tok),
                   pl.BlockSpec((1, ts, aw), tok),
                   pl.BlockSpec((1, nheads, ts // PAIR_KEYS, VT_ROWS, PAIR_KEYS),
                                lambda bi, si: (bi, 0, si, 0, 0)),
                   pl.BlockSpec((1, ts, pw), tok)],
        out_shape=[jax.ShapeDtypeStruct((b, s, aw), BF16),
                   jax.ShapeDtypeStruct((b, s, aw), BF16),
                   jax.ShapeDtypeStruct((b, nheads, s // PAIR_KEYS, VT_ROWS, PAIR_KEYS), BF16),
                   jax.ShapeDtypeStruct((b, s, pw), BF16)],
        scratch_shapes=[pltpu.VMEM((POOL_HALO + ts, pw), F32)],
        compiler_params=_params("arbitrary", "arbitrary"),
        name="in_proj",
    )(x, sc, sh, g, w_in, cos_t, sa_t, sb_t, w_pool, pool_scale, pool_gain)


def _attn_kernel(q_ref, k_ref, vt_ref, o_ref, km_ref, bias_ref, s0_ref, s1_ref, *, nb):
    s_slots = (s0_ref, s1_ref)
    n = pl.program_id(2)
    blk = MOBA_BLOCK
    heads = range(HEADS_PER_GROUP)

    @pl.when(n == 0)
    def _():
        for j in range(nb):
            kj = k_ref[0, j * blk:(j + 1) * blk, :].astype(F32)
            km_ref[j:j + 1, :] = jnp.mean(kj, axis=0, keepdims=True)

    q2 = q_ref[0]
    lane = lax.broadcasted_iota(jnp.int32, (1, LANES), 1)
    qh = [jnp.where((lane >= hh * HEAD_DIM) & (lane < (hh + 1) * HEAD_DIM), q2, jnp.zeros_like(q2))
          for hh in heads]

    npairs = nb // 2
    trips = (n + 1) // 2

    def issue_scores(pi, slot):
        kp = k_ref[0, pl.ds(pl.multiple_of(pi * PAIR_KEYS, PAIR_KEYS), PAIR_KEYS), :]
        for hh in heads:
            s_slots[slot][hh] = lax.dot_general(kp, qh[hh], _NT, preferred_element_type=F32)

    km = km_ref[...]
    km_hi = km.astype(BF16)
    km_lo = (km - km_hi.astype(F32)).astype(BF16)
    kd = k_ref[0, pl.ds(pl.multiple_of(n * blk, blk), blk), :]
    lhs = jnp.concatenate([km_hi, km_lo, kd, k_ref[0, 0:PAIR_KEYS, :]], axis=0)
    jidx = lax.broadcasted_iota(jnp.int32, (nb, 1), 0)
    past = jidx < n
    s_own = []
    for hh in heads:
        r = lax.dot_general(lhs, qh[hh], _NT, preferred_element_type=F32)
        gate = r[0:nb] + r[nb:2 * nb]
        s_own.append(r[2 * nb:2 * nb + blk])
        s_slots[0][hh] = r[2 * nb + blk:]
        gate = jnp.where(past, gate, NEG_INF)
        rank = jnp.zeros(gate.shape, jnp.int32)
        for i in range(nb):
            gi = gate[i:i + 1, :]
            beats = (gi > gate) | ((gi == gate) & (i < jidx))
            rank = rank + beats.astype(jnp.int32)
        sel = (rank < MOBA_TOP_K) & past
        bias = jnp.where(sel, 0.0, NEG_INF).astype(F32)
        for j in range(nb):
            bias_ref[hh, j] = bias[j:j + 1, :]

    causal = (lax.broadcasted_iota(jnp.int32, (blk, blk), 0)
              <= lax.broadcasted_iota(jnp.int32, (blk, blk), 1))
    own_half = pl.ds(pl.multiple_of((n % 2) * blk, blk), blk)
    init = []
    for hh in heads:
        s = jnp.where(causal, s_own[hh], NEG_INF)
        m = jnp.max(s, axis=0, keepdims=True)
        p = jnp.exp2(s - m).astype(BF16)
        init.append((m, jnp.dot(vt_ref[0, hh, n // 2, :, own_half], p, preferred_element_type=F32)))

    def trip(i, carry, slot, prefetch=True):
        if prefetch:
            issue_scores(jnp.minimum(i + 1, npairs - 1), 1 - slot)
        out = []
        for hh in heads:
            m, acc = carry[hh]
            b0 = bias_ref[hh, 2 * i]
            b1 = bias_ref[hh, 2 * i + 1]
            s0 = s_slots[slot][hh, 0:blk, :]
            s1 = s_slots[slot][hh, blk:2 * blk, :]
            m_new = jnp.maximum(m, jnp.maximum(jnp.max(s0, axis=0, keepdims=True) + b0,
                                               jnp.max(s1, axis=0, keepdims=True) + b1))
            p = jnp.concatenate([jnp.exp2(s0 - (m_new - b0)).astype(BF16),
                                 jnp.exp2(s1 - (m_new - b1)).astype(BF16)], axis=0)
            acc = jnp.exp2(m - m_new) * acc + jnp.dot(vt_ref[0, hh, i], p, preferred_element_type=F32)
            out.append((m_new, acc))
        return tuple(out)

    def two_trips(u, carry):
        return trip(2 * u + 1, trip(2 * u, carry, 0), 1)

    carry = lax.fori_loop(0, trips // 2, two_trips, tuple(init))
    carry = lax.cond(trips % 2 == 1, lambda: trip(trips - 1, carry, 0, prefetch=False),
                     lambda: carry)
    outs = []
    for hh in heads:
        acc = carry[hh][1]
        outs.append(acc[0:HEAD_DIM, :] / acc[HEAD_DIM:HEAD_DIM + 1, :])
    o_ref[0] = jnp.concatenate(outs, axis=0).T


def _attn_call(q, k, vt):
    b, s, aw = q.shape
    ngroups = aw // LANES
    nb = s // MOBA_BLOCK
    return pl.pallas_call(
        functools.partial(_attn_kernel, nb=nb),
        grid=(b, ngroups, nb),
        in_specs=[pl.BlockSpec((1, MOBA_BLOCK, LANES), lambda bi, gi, n: (bi, n, gi)),
                  pl.BlockSpec((1, s, LANES), lambda bi, gi, n: (bi, 0, gi)),
                  pl.BlockSpec((1, HEADS_PER_GROUP, nb // 2, VT_ROWS, PAIR_KEYS),
                               lambda bi, gi, n: (bi, gi, 0, 0, 0))],
        out_specs=pl.BlockSpec((1, MOBA_BLOCK, LANES), lambda bi, gi, n: (bi, n, gi)),
        out_shape=jax.ShapeDtypeStruct((b, s, aw), F32),
        scratch_shapes=[pltpu.VMEM((nb, LANES), F32),
                        pltpu.VMEM((HEADS_PER_GROUP, nb, 1, MOBA_BLOCK), F32),
                        pltpu.VMEM((HEADS_PER_GROUP, PAIR_KEYS, MOBA_BLOCK), F32),
                        pltpu.VMEM((HEADS_PER_GROUP, PAIR_KEYS, MOBA_BLOCK), F32)],
        compiler_params=_params("arbitrary", "arbitrary", "arbitrary"),
        name="moba_attn",
    )(q, k, vt)


def _out_kernel(ao_ref, pn_ref, x_ref, ga_ref, w_ref, gp_ref, gt_ref, o_ref, *, aw):
    an = (_rms(ao_ref[0]) * ga_ref[...]).astype(BF16)
    y = (jnp.dot(an, w_ref[0:aw, :], preferred_element_type=F32)
         + jnp.dot(pn_ref[0], w_ref[aw:, :], preferred_element_type=F32))
    o_ref[0] = x_ref[0] + gt_ref[0] * (_rms(y) * gp_ref[...])


def _out_call(ao, pn, x, attn_gain, w_out, g_post, gt, ts=512):
    b, s, d = x.shape
    aw = ao.shape[2]
    pw = pn.shape[2]
    tok = lambda bi, si: (bi, si, 0)
    fix2 = lambda bi, si: (0, 0)
    return pl.pallas_call(
        functools.partial(_out_kernel, aw=aw),
        grid=(b, s // ts),
        in_specs=[pl.BlockSpec((1, ts, aw), tok),
                  pl.BlockSpec((1, ts, pw), tok),
                  pl.BlockSpec((1, ts, d), tok),
                  pl.BlockSpec((1, aw), fix2),
                  pl.BlockSpec(w_out.shape, fix2),
                  pl.BlockSpec((1, d), fix2),
                  pl.BlockSpec((1, 1, d), lambda bi, si: (bi, 0, 0))],
        out_specs=pl.BlockSpec((1, ts, d), tok),
        out_shape=jax.ShapeDtypeStruct((b, s, d), F32),
        compiler_params=_params("arbitrary", "arbitrary"),
        name="out_proj",
    )(ao, pn, x, attn_gain, w_out, g_post, gt)


def _ffn_kernel(x_ref, sc_ref, sh_ref, g_ref, wu_ref, cw_ref, cb_ref, wd_ref, gp_ref, gt_ref,
                o_ref, tail_ref, act_ref, *, ts, tf):
    si = pl.program_id(1)
    x = x_ref[0]
    hb = ((_rms(x) * g_ref[...]) * (1.0 + sc_ref[0]) + sh_ref[0]).astype(BF16)
    d_ff = wd_ref.shape[0]
    first = si == 0

    def conv(lo):
        u = jnp.dot(hb, wu_ref[:, lo:lo + tf], preferred_element_type=F32)
        tail = jnp.where(first, 0.0, tail_ref[:, lo:lo + tf])
        tail_ref[:, lo:lo + tf] = u[ts - CONV_HALO:, :]
        ucat = jnp.concatenate([tail, u], axis=0)
        uc = cb_ref[:, lo:lo + tf] + cw_ref[CONV_WIDTH - 1:CONV_WIDTH, lo:lo + tf] * u
        for j in range(CONV_WIDTH - 1):
            back = CONV_WIDTH - 1 - j
            uc = uc + cw_ref[j:j + 1, lo:lo + tf] * ucat[CONV_HALO - back:CONV_HALO - back + ts, :]
        return uc

    for fi in range(d_ff // tf):
        a = conv(fi * tf)
        g = conv(d_ff + fi * tf)
        act_ref[:, fi * tf:(fi + 1) * tf] = (a * jax.nn.sigmoid(a) * g).astype(BF16)
    y = jnp.dot(act_ref[...], wd_ref[...], preferred_element_type=F32)
    o_ref[0] = x + gt_ref[0] * (_rms(y) * gp_ref[...])


def _ffn_call(x, sc, sh, g_pre, w_up, conv_w, conv_b, w_down, g_post, gt, ts=512, tf=256):
    b, s, d = x.shape
    f = w_down.shape[0]
    tok = lambda bi, si: (bi, si, 0)
    row = lambda bi, si: (bi, 0, 0)
    fix2 = lambda bi, si: (0, 0)
    resident = dict(pipeline_mode=pl.Buffered(1))
    return pl.pallas_call(
        functools.partial(_ffn_kernel, ts=ts, tf=tf),
        grid=(b, s // ts),
        in_specs=[pl.BlockSpec((1, ts, d), tok),
                  pl.BlockSpec((1, 1, d), row),
                  pl.BlockSpec((1, 1, d), row),
                  pl.BlockSpec((1, d), fix2),
                  pl.BlockSpec(w_up.shape, fix2, **resident),
                  pl.BlockSpec(conv_w.shape, fix2),
                  pl.BlockSpec((1, 2 * f), fix2),
                  pl.BlockSpec(w_down.shape, fix2, **resident),
                  pl.BlockSpec((1, d), fix2),
                  pl.BlockSpec((1, 1, d), row)],
        out_specs=pl.BlockSpec((1, ts, d), tok),
        out_shape=jax.ShapeDtypeStruct((b, s, d), F32),
        scratch_shapes=[pltpu.VMEM((CONV_HALO, 2 * f), F32),
                        pltpu.VMEM((ts, f), BF16)],
        compiler_params=_params("arbitrary", "arbitrary"),
        name="conv_ffn",
    )(x, sc, sh, g_pre, w_up, conv_w, conv_b[None], w_down, g_post, gt)


def _rotary_tables(s):
    half = ROT_DIM // 2
    pos = jnp.arange(s, dtype=F32)
    inv_freq = jnp.power(jnp.float32(ROPE_THETA), -jnp.arange(0, ROT_DIM, 2, dtype=F32) / ROT_DIM)
    ang = pos[:, None] * inv_freq[None, :]
    cos = jnp.cos(ang)
    sin = jnp.sin(ang)
    ones = jnp.ones((s, HEAD_DIM - ROT_DIM), F32)
    zeros = jnp.zeros((s, HEAD_DIM - ROT_DIM), F32)
    zh = jnp.zeros((s, half), F32)
    cos_h = jnp.concatenate([cos, cos, ones], axis=1)
    sa_h = jnp.concatenate([-sin, zh, zeros], axis=1)
    sb_h = jnp.concatenate([zh, sin, zeros], axis=1)
    rep = lambda t: jnp.concatenate([t] * HEADS_PER_GROUP, axis=1)
    return rep(cos_h), rep(sa_h), rep(sb_h)


def kernel(x, c, w_ada, b_ada, g_pre_mix, w_in, w_pool, pool_scale, attn_out_gain, pool_out_gain,
           w_out, g_post_mix, g_pre_ffn, w_up, conv_w, conv_b, w_down, g_post_ffn):
    b, s, d = x.shape
    depth = w_ada.shape[0]
    assert s % PAIR_KEYS == 0 and d % LANES == 0
    mod = _mod_call(c, w_ada, b_ada)
    cos_t, sa_t, sb_t = _rotary_tables(s)
    for l in range(depth):
        sh1, sc1, gt1, sh2, sc2, gt2 = (mod[l, :, i * d:(i + 1) * d].reshape(b, 1, d)
                                        for i in range(6))
        q, k, vt, pn = _in_call(x, sc1, sh1, g_pre_mix[l][None], w_in[l].astype(BF16),
                                cos_t, sa_t, sb_t, w_pool[l].astype(BF16),
                                pool_scale[l][None], pool_out_gain[l][None])
        ao = _attn_call(q, k, vt)
        x = _out_call(ao, pn, x, attn_out_gain[l][None], w_out[l].astype(BF16),
                      g_post_mix[l][None], gt1)
        x = _ffn_call(x, sc2, sh2, g_pre_ffn[l][None], w_up[l].astype(BF16), conv_w[l], conv_b[l],
                      w_down[l].astype(BF16), g_post_ffn[l][None], gt2)
    return x
```

```python
import functools

import jax
import jax.numpy as jnp
from jax import lax
from jax.experimental import pallas as pl
from jax.experimental.pallas import tpu as pltpu

F32 = jnp.float32
BF16 = jnp.bfloat16

HEAD_DIM = 64
HEADS_PER_GROUP = 2
LANES = 128
MOBA_BLOCK = 256
MOBA_TOP_K = 3
POOL_WINDOWS = (2, 4, 8, 16)
POOL_HALO = 16
ROT_DIM = HEAD_DIM // 4
ROPE_THETA = 500000.0
CONV_WIDTH = 3
CONV_HALO = 8
NORM_EPS = 1e-6
NEG_INF = -1e30
LOG2_E = 1.4426950408889634
VT_ROWS = HEAD_DIM + 16
PAIR_KEYS = 2 * MOBA_BLOCK
VMEM_LIMIT = 48 * 1024 * 1024

_NT = (((1,), (1,)), ((), ()))


def _rms(x):
    return x * lax.rsqrt(jnp.mean(x * x, axis=-1, keepdims=True) + NORM_EPS)


def _params(*sem):
    return pltpu.CompilerParams(dimension_semantics=sem, vmem_limit_bytes=VMEM_LIMIT)


def _mod_kernel(c_ref, w_ref, b_ref, o_ref):
    c = c_ref[...]
    c_act = c * jax.nn.sigmoid(c)
    o_ref[0] = jnp.dot(c_act.astype(BF16), w_ref[0].astype(BF16),
                       preferred_element_type=F32) + b_ref[0]


def _mod_call(c, w_ada, b_ada, tn=1536):
    nl, d, n6 = w_ada.shape
    b = c.shape[0]
    return pl.pallas_call(
        _mod_kernel,
        grid=(nl, n6 // tn),
        in_specs=[pl.BlockSpec((b, d), lambda l, j: (0, 0)),
                  pl.BlockSpec((1, d, tn), lambda l, j: (l, 0, j)),
                  pl.BlockSpec((1, 1, tn), lambda l, j: (l, 0, j))],
        out_specs=pl.BlockSpec((1, b, tn), lambda l, j: (l, 0, j)),
        out_shape=jax.ShapeDtypeStruct((nl, b, n6), F32),
        compiler_params=_params("arbitrary", "arbitrary"),
        name="adaln_mod",
    )(c, w_ada, b_ada.reshape(nl, 1, n6))


def _in_kernel(x_ref, sc_ref, sh_ref, g_ref, w_ref, cos_ref, sa_ref, sb_ref, wp_ref, ps_ref,
               pg_ref, q_ref, k_ref, vt_ref, pn_ref, halo_ref, *, ts, aw):
    si = pl.program_id(1)
    h = _rms(x_ref[0]) * g_ref[...]
    h = h * (1.0 + sc_ref[0]) + sh_ref[0]
    z = jnp.dot(h.astype(BF16), w_ref[...], preferred_element_type=F32)

    cos = cos_ref[...]
    sa = sa_ref[...]
    sb = sb_ref[...]

    def rot(t):
        return t * cos + pltpu.roll(t, LANES - ROT_DIM // 2, 1) * sa + pltpu.roll(t, ROT_DIM // 2, 1) * sb

    scale = HEAD_DIM ** -0.5 * LOG2_E
    ngroups = aw // LANES
    ones = jnp.ones((VT_ROWS - HEAD_DIM, PAIR_KEYS), BF16)
    for gi in range(ngroups):
        lo = gi * LANES
        q_ref[0, :, lo:lo + LANES] = (rot(z[:, lo:lo + LANES]) * scale).astype(BF16)
        k_ref[0, :, lo:lo + LANES] = rot(z[:, aw + lo:aw + lo + LANES]).astype(BF16)
        vt = z[:, 2 * aw + lo:2 * aw + lo + LANES].T.astype(BF16)
        for hh in range(HEADS_PER_GROUP):
            head = gi * HEADS_PER_GROUP + hh
            for jp in range(ts // PAIR_KEYS):
                vt_ref[0, head, jp, 0:HEAD_DIM, :] = vt[hh * HEAD_DIM:(hh + 1) * HEAD_DIM,
                                                        jp * PAIR_KEYS:(jp + 1) * PAIR_KEYS]
                vt_ref[0, head, jp, HEAD_DIM:VT_ROWS, :] = ones

    pz = z[:, 3 * aw:]

    @pl.when(si == 0)
    def _():
        halo_ref[0:POOL_HALO, :] = jnp.zeros((POOL_HALO, pz.shape[1]), F32)

    halo_ref[POOL_HALO:POOL_HALO + ts, :] = pz
    tpos = si * ts + lax.broadcasted_iota(jnp.int32, (ts, 1), 0)
    outs = []
    for g, w in enumerate(POOL_WINDOWS):
        lo = g * LANES
        s = pz[:, lo:lo + LANES]
        for i in range(1, w):
            s = s + halo_ref[POOL_HALO - i:POOL_HALO - i + ts, lo:lo + LANES]
        cnt = jnp.minimum(tpos + 1, w).astype(F32)
        pooled = s / cnt - pz[:, lo:lo + LANES]
        outs.append(jnp.dot(pooled.astype(BF16), wp_ref[g], preferred_element_type=F32))
    halo_ref[0:POOL_HALO, :] = pz[ts - POOL_HALO:, :]
    po = jnp.concatenate(outs, axis=1) * ps_ref[...]
    pn_ref[0] = (_rms(po) * pg_ref[...]).astype(BF16)


def _in_call(x, sc, sh, g, w_in, cos_t, sa_t, sb_t, w_pool, pool_scale, pool_gain, ts=512):
    b, s, d = x.shape
    n_in = w_in.shape[1]
    pw = w_pool.shape[0] * w_pool.shape[1]
    aw = (n_in - pw) // 3
    nheads = aw // HEAD_DIM
    tok = lambda bi, si: (bi, si, 0)
    row = lambda bi, si: (bi, 0, 0)
    fix2 = lambda bi, si: (0, 0)
    return pl.pallas_call(
        functools.partial(_in_kernel, ts=ts, aw=aw),
        grid=(b, s // ts),
        in_specs=[pl.BlockSpec((1, ts, d), tok),
                  pl.BlockSpec((1, 1, d), row),
                  pl.BlockSpec((1, 1, d), row),
                  pl.BlockSpec((1, d), fix2),
                  pl.BlockSpec((d, n_in), fix2),
                  pl.BlockSpec((ts, LANES), lambda bi, si: (si, 0)),
                  pl.BlockSpec((ts, LANES), lambda bi, si: (si, 0)),
                  pl.BlockSpec((ts, LANES), lambda bi, si: (si, 0)),
                  pl.BlockSpec(w_pool.shape, lambda bi, si: (0, 0, 0)),
                  pl.BlockSpec((1, pw), fix2),
                  pl.BlockSpec((1, pw), fix2)],
        out_specs=[pl.BlockSpec((1, ts, aw), tok),
                   pl.BlockSpec((1, ts, aw), tok),
                   pl.BlockSpec((1, nheads, ts // PAIR_KEYS, VT_ROWS, PAIR_KEYS),
                                lambda bi, si: (bi, 0, si, 0, 0)),
                   pl.BlockSpec((1, ts, pw), tok)],
        out_shape=[jax.ShapeDtypeStruct((b, s, aw), BF16),
                   jax.ShapeDtypeStruct((b, s, aw), BF16),
                   jax.ShapeDtypeStruct((b, nheads, s // PAIR_KEYS, VT_ROWS, PAIR_KEYS), BF16),
                   jax.ShapeDtypeStruct((b, s, pw), BF16)],
        scratch_shapes=[pltpu.VMEM((POOL_HALO + ts, pw), F32)],
        compiler_params=_params("arbitrary", "arbitrary"),
        name="in_proj",
    )(x, sc, sh, g, w_in, cos_t, sa_t, sb_t, w_pool, pool_scale, pool_gain)


def _attn_kernel(q_ref, k_ref, vt_ref, o_ref, km_ref, bias_ref, s0_ref, s1_ref, *, nb):
    s_slots = (s0_ref, s1_ref)
    t = pl.program_id(2)
    blk = MOBA_BLOCK
    tq = PAIR_KEYS
    heads = range(HEADS_PER_GROUP)

    @pl.when(t == 0)
    def _():
        for j in range(nb):
            kj = k_ref[0, j * blk:(j + 1) * blk, :].astype(F32)
            km_ref[j:j + 1, :] = jnp.mean(kj, axis=0, keepdims=True)

    q2 = q_ref[0]
    lane = lax.broadcasted_iota(jnp.int32, (1, LANES), 1)
    qh = [jnp.where((lane >= hh * HEAD_DIM) & (lane < (hh + 1) * HEAD_DIM), q2, jnp.zeros_like(q2))
          for hh in heads]

    def pair_keys(pi):
        return k_ref[0, pl.ds(pl.multiple_of(pi * PAIR_KEYS, PAIR_KEYS), PAIR_KEYS), :]

    def issue_scores(pi, slot):
        kp = pair_keys(pi)
        for hh in heads:
            s_slots[slot][hh] = lax.dot_general(kp, qh[hh], _NT, preferred_element_type=F32)

    def softmax_pv(hh, s, pi, state):
        b0 = bias_ref[hh, 2 * pi]
        b1 = bias_ref[hh, 2 * pi + 1]
        s0 = s[0:blk]
        s1 = s[blk:2 * blk]
        m_new = jnp.maximum(jnp.max(s0, axis=0, keepdims=True) + b0,
                            jnp.max(s1, axis=0, keepdims=True) + b1)
        if state is not None:
            m_new = jnp.maximum(state[0], m_new)
        p = jnp.concatenate([jnp.exp2(s0 - (m_new - b0)).astype(BF16),
                             jnp.exp2(s1 - (m_new - b1)).astype(BF16)], axis=0)
        acc = jnp.dot(vt_ref[0, hh, pi], p, preferred_element_type=F32)
        if state is not None:
            acc = jnp.exp2(state[0] - m_new) * state[1] + acc
        return m_new, acc

    km = km_ref[...]
    km_hi = km.astype(BF16)
    km_lo = (km - km_hi.astype(F32)).astype(BF16)
    lhs = jnp.concatenate([km_hi, km_lo, pair_keys(t)], axis=0)
    r = [lax.dot_general(lhs, qh[hh], _NT, preferred_element_type=F32) for hh in heads]
    issue_scores(0, 0)

    jidx = lax.broadcasted_iota(jnp.int32, (nb, 1), 0)
    qblk = 2 * t + lax.broadcasted_iota(jnp.int32, (1, tq), 1) // blk
    past = jidx < qblk
    for hh in heads:
        gate = jnp.where(past, r[hh][0:nb] + r[hh][nb:2 * nb], NEG_INF)
        rank = jnp.zeros(gate.shape, jnp.int32)
        for i in range(nb):
            gi = gate[i:i + 1, :]
            beats = (gi > gate) | ((gi == gate) & (i < jidx))
            rank = rank + beats.astype(jnp.int32)
        allowed = ((rank < MOBA_TOP_K) & past) | (jidx == qblk)
        bias = jnp.where(allowed, 0.0, NEG_INF).astype(F32)
        for j in range(nb):
            bias_ref[hh, j] = bias[j:j + 1, :]

    causal = (lax.broadcasted_iota(jnp.int32, (tq, tq), 0)
              <= lax.broadcasted_iota(jnp.int32, (tq, tq), 1))
    state = tuple(softmax_pv(hh, jnp.where(causal, r[hh][2 * nb:], NEG_INF), t, None)
                  for hh in heads)

    def trip(i, state, slot, prefetch=True):
        if prefetch:
            issue_scores(jnp.minimum(i + 1, nb // 2 - 1), 1 - slot)
        return tuple(softmax_pv(hh, s_slots[slot][hh], i, state[hh]) for hh in heads)

    def two_trips(u, state):
        return trip(2 * u + 1, trip(2 * u, state, 0), 1)

    state = lax.fori_loop(0, t // 2, two_trips, state)
    state = lax.cond(t % 2 == 1, lambda: trip(t - 1, state, 0, prefetch=False), lambda: state)
    outs = [acc[0:HEAD_DIM, :] / acc[HEAD_DIM:HEAD_DIM + 1, :] for _, acc in state]
    o_ref[0] = jnp.concatenate(outs, axis=0).T


def _attn_call(q, k, vt):
    b, s, aw = q.shape
    ngroups = aw // LANES
    nb = s // MOBA_BLOCK
    tq = PAIR_KEYS
    return pl.pallas_call(
        functools.partial(_attn_kernel, nb=nb),
        grid=(b, ngroups, s // tq),
        in_specs=[pl.BlockSpec((1, tq, LANES), lambda bi, gi, t: (bi, t, gi)),
                  pl.BlockSpec((1, s, LANES), lambda bi, gi, t: (bi, 0, gi)),
                  pl.BlockSpec((1, HEADS_PER_GROUP, nb // 2, VT_ROWS, PAIR_KEYS),
                               lambda bi, gi, t: (bi, gi, 0, 0, 0))],
        out_specs=pl.BlockSpec((1, tq, LANES), lambda bi, gi, t: (bi, t, gi)),
        out_shape=jax.ShapeDtypeStruct((b, s, aw), F32),
        scratch_shapes=[pltpu.VMEM((nb, LANES), F32),
                        pltpu.VMEM((HEADS_PER_GROUP, nb, 1, tq), F32),
                        pltpu.VMEM((HEADS_PER_GROUP, PAIR_KEYS, tq), F32),
                        pltpu.VMEM((HEADS_PER_GROUP, PAIR_KEYS, tq), F32)],
        compiler_params=_params("arbitrary", "arbitrary", "arbitrary"),
        name="moba_attn",
    )(q, k, vt)


def _out_kernel(ao_ref, pn_ref, x_ref, ga_ref, w_ref, gp_ref, gt_ref, o_ref, *, aw):
    an = (_rms(ao_ref[0]) * ga_ref[...]).astype(BF16)
    y = (jnp.dot(an, w_ref[0:aw, :], preferred_element_type=F32)
         + jnp.dot(pn_ref[0], w_ref[aw:, :], preferred_element_type=F32))
    o_ref[0] = x_ref[0] + gt_ref[0] * (_rms(y) * gp_ref[...])


def _out_call(ao, pn, x, attn_gain, w_out, g_post, gt, ts=512):
    b, s, d = x.shape
    aw = ao.shape[2]
    pw = pn.shape[2]
    tok = lambda bi, si: (bi, si, 0)
    fix2 = lambda bi, si: (0, 0)
    return pl.pallas_call(
        functools.partial(_out_kernel, aw=aw),
        grid=(b, s // ts),
        in_specs=[pl.BlockSpec((1, ts, aw), tok),
                  pl.BlockSpec((1, ts, pw), tok),
                  pl.BlockSpec((1, ts, d), tok),
                  pl.BlockSpec((1, aw), fix2),
                  pl.BlockSpec(w_out.shape, fix2),
                  pl.BlockSpec((1, d), fix2),
                  pl.BlockSpec((1, 1, d), lambda bi, si: (bi, 0, 0))],
        out_specs=pl.BlockSpec((1, ts, d), tok),
        out_shape=jax.ShapeDtypeStruct((b, s, d), F32),
        compiler_params=_params("arbitrary", "arbitrary"),
        name="out_proj",
    )(ao, pn, x, attn_gain, w_out, g_post, gt)


def _ffn_kernel(x_ref, sc_ref, sh_ref, g_ref, wu_ref, cw_ref, cb_ref, wd_ref, gp_ref, gt_ref,
                o_ref, tail_ref, act_ref, *, ts, tf):
    si = pl.program_id(1)
    x = x_ref[0]
    hb = ((_rms(x) * g_ref[...]) * (1.0 + sc_ref[0]) + sh_ref[0]).astype(BF16)
    d_ff = wd_ref.shape[0]
    first = si == 0

    def conv(lo):
        u = jnp.dot(hb, wu_ref[:, lo:lo + tf], preferred_element_type=F32)
        tail = jnp.where(first, 0.0, tail_ref[:, lo:lo + tf])
        tail_ref[:, lo:lo + tf] = u[ts - CONV_HALO:, :]
        ucat = jnp.concatenate([tail, u], axis=0)
        uc = cb_ref[:, lo:lo + tf] + cw_ref[CONV_WIDTH - 1:CONV_WIDTH, lo:lo + tf] * u
        for j in range(CONV_WIDTH - 1):
            back = CONV_WIDTH - 1 - j
            uc = uc + cw_ref[j:j + 1, lo:lo + tf] * ucat[CONV_HALO - back:CONV_HALO - back + ts, :]
        return uc

    for fi in range(d_ff // tf):
        a = conv(fi * tf)
        g = conv(d_ff + fi * tf)
        act_ref[:, fi * tf:(fi + 1) * tf] = (a * jax.nn.sigmoid(a) * g).astype(BF16)
    y = jnp.dot(act_ref[...], wd_ref[...], preferred_element_type=F32)
    o_ref[0] = x + gt_ref[0] * (_rms(y) * gp_ref[...])


def _ffn_call(x, sc, sh, g_pre, w_up, conv_w, conv_b, w_down, g_post, gt, ts=512, tf=256):
    b, s, d = x.shape
    f = w_down.shape[0]
    tok = lambda bi, si: (bi, si, 0)
    row = lambda bi, si: (bi, 0, 0)
    fix2 = lambda bi, si: (0, 0)
    resident = dict(pipeline_mode=pl.Buffered(1))
    return pl.pallas_call(
        functools.partial(_ffn_kernel, ts=ts, tf=tf),
        grid=(b, s // ts),
        in_specs=[pl.BlockSpec((1, ts, d), tok),
                  pl.BlockSpec((1, 1, d), row),
                  pl.BlockSpec((1, 1, d), row),
                  pl.BlockSpec((1, d), fix2),
                  pl.BlockSpec(w_up.shape, fix2, **resident),
                  pl.BlockSpec(conv_w.shape, fix2),
                  pl.BlockSpec((1, 2 * f), fix2),
                  pl.BlockSpec(w_down.shape, fix2, **resident),
                  pl.BlockSpec((1, d), fix2),
                  pl.BlockSpec((1, 1, d), row)],
        out_specs=pl.BlockSpec((1, ts, d), tok),
        out_shape=jax.ShapeDtypeStruct((b, s, d), F32),
        scratch_shapes=[pltpu.VMEM((CONV_HALO, 2 * f), F32),
                        pltpu.VMEM((ts, f), BF16)],
        compiler_params=_params("arbitrary", "arbitrary"),
        name="conv_ffn",
    )(x, sc, sh, g_pre, w_up, conv_w, conv_b[None], w_down, g_post, gt)


def _rotary_tables(s):
    half = ROT_DIM // 2
    pos = jnp.arange(s, dtype=F32)
    inv_freq = jnp.power(jnp.float32(ROPE_THETA), -jnp.arange(0, ROT_DIM, 2, dtype=F32) / ROT_DIM)
    ang = pos[:, None] * inv_freq[None, :]
    cos = jnp.cos(ang)
    sin = jnp.sin(ang)
    ones = jnp.ones((s, HEAD_DIM - ROT_DIM), F32)
    zeros = jnp.zeros((s, HEAD_DIM - ROT_DIM), F32)
    zh = jnp.zeros((s, half), F32)
    cos_h = jnp.concatenate([cos, cos, ones], axis=1)
    sa_h = jnp.concatenate([-sin, zh, zeros], axis=1)
    sb_h = jnp.concatenate([zh, sin, zeros], axis=1)
    rep = lambda t: jnp.concatenate([t] * HEADS_PER_GROUP, axis=1)
    return rep(cos_h), rep(sa_h), rep(sb_h)


def kernel(x, c, w_ada, b_ada, g_pre_mix, w_in, w_pool, pool_scale, attn_out_gain, pool_out_gain,
           w_out, g_post_mix, g_pre_ffn, w_up, conv_w, conv_b, w_down, g_post_ffn):
    b, s, d = x.shape
    depth = w_ada.shape[0]
    assert s % PAIR_KEYS == 0 and d % LANES == 0
    mod = _mod_call(c, w_ada, b_ada)
    cos_t, sa_t, sb_t = _rotary_tables(s)
    for l in range(depth):
        sh1, sc1, gt1, sh2, sc2, gt2 = (mod[l, :, i * d:(i + 1) * d].reshape(b, 1, d)
                                        for i in range(6))
        q, k, vt, pn = _in_call(x, sc1, sh1, g_pre_mix[l][None], w_in[l].astype(BF16),
                                cos_t, sa_t, sb_t, w_pool[l].astype(BF16),
                                pool_scale[l][None], pool_out_gain[l][None])
        ao = _attn_call(q, k, vt)
        x = _out_call(ao, pn, x, attn_out_gain[l][None], w_out[l].astype(BF16),
                      g_post_mix[l][None], gt1)
        x = _ffn_call(x, sc2, sh2, g_pre_ffn[l][None], w_up[l].astype(BF16), conv_w[l], conv_b[l],
                      w_down[l].astype(BF16), g_post_ffn[l][None], gt2)
    return x
```

```python
import functools

import jax
import jax.numpy as jnp
from jax import lax
from jax.experimental import pallas as pl
from jax.experimental.pallas import tpu as pltpu

F32 = jnp.float32
BF16 = jnp.bfloat16

HEAD_DIM = 64
HEADS_PER_GROUP = 2
LANES = 128
MOBA_BLOCK = 256
MOBA_TOP_K = 3
POOL_WINDOWS = (2, 4, 8, 16)
POOL_HALO = 16
ROT_DIM = HEAD_DIM // 4
ROPE_THETA = 500000.0
CONV_WIDTH = 3
CONV_HALO = 8
NORM_EPS = 1e-6
NEG_INF = -1e30
LOG2_E = 1.4426950408889634
VT_ROWS = HEAD_DIM + 16
PAIR_KEYS = 2 * MOBA_BLOCK
VMEM_LIMIT = 48 * 1024 * 1024

_NT = (((1,), (1,)), ((), ()))


def _rms(x):
    return x * lax.rsqrt(jnp.mean(x * x, axis=-1, keepdims=True) + NORM_EPS)


def _params(*sem):
    return pltpu.CompilerParams(dimension_semantics=sem, vmem_limit_bytes=VMEM_LIMIT)


def _mod_kernel(c_ref, w_ref, b_ref, o_ref):
    c = c_ref[...]
    c_act = c * jax.nn.sigmoid(c)
    o_ref[0] = jnp.dot(c_act.astype(BF16), w_ref[0].astype(BF16),
                       preferred_element_type=F32) + b_ref[0]


def _mod_call(c, w_ada, b_ada, tn=1536):
    nl, d, n6 = w_ada.shape
    b = c.shape[0]
    return pl.pallas_call(
        _mod_kernel,
        grid=(nl, n6 // tn),
        in_specs=[pl.BlockSpec((b, d), lambda l, j: (0, 0)),
                  pl.BlockSpec((1, d, tn), lambda l, j: (l, 0, j)),
                  pl.BlockSpec((1, 1, tn), lambda l, j: (l, 0, j))],
        out_specs=pl.BlockSpec((1, b, tn), lambda l, j: (l, 0, j)),
        out_shape=jax.ShapeDtypeStruct((nl, b, n6), F32),
        compiler_params=_params("arbitrary", "arbitrary"),
        name="adaln_mod",
    )(c, w_ada, b_ada.reshape(nl, 1, n6))


def _in_kernel(x_ref, sc_ref, sh_ref, g_ref, w_ref, cos_ref, sa_ref, sb_ref, wp_ref, ps_ref,
               pg_ref, q_ref, k_ref, vt_ref, pn_ref, halo_ref, *, ts, aw):
    si = pl.program_id(1)
    h = _rms(x_ref[0]) * g_ref[...]
    h = h * (1.0 + sc_ref[0]) + sh_ref[0]
    z = jnp.dot(h.astype(BF16), w_ref[...], preferred_element_type=F32)

    cos = cos_ref[...]
    sa = sa_ref[...]
    sb = sb_ref[...]

    def rot(t):
        return t * cos + pltpu.roll(t, LANES - ROT_DIM // 2, 1) * sa + pltpu.roll(t, ROT_DIM // 2, 1) * sb

    scale = HEAD_DIM ** -0.5 * LOG2_E
    ngroups = aw // LANES
    ones = jnp.ones((VT_ROWS - HEAD_DIM, PAIR_KEYS), BF16)
    for gi in range(ngroups):
        lo = gi * LANES
        q_ref[0, :, lo:lo + LANES] = (rot(z[:, lo:lo + LANES]) * scale).astype(BF16)
        k_ref[0, :, lo:lo + LANES] = rot(z[:, aw + lo:aw + lo + LANES]).astype(BF16)
        vt = z[:, 2 * aw + lo:2 * aw + lo + LANES].T.astype(BF16)
        for hh in range(HEADS_PER_GROUP):
            head = gi * HEADS_PER_GROUP + hh
            for jp in range(ts // PAIR_KEYS):
                vt_ref[0, head, jp, 0:HEAD_DIM, :] = vt[hh * HEAD_DIM:(hh + 1) * HEAD_DIM,
                                                        jp * PAIR_KEYS:(jp + 1) * PAIR_KEYS]
                vt_ref[0, head, jp, HEAD_DIM:VT_ROWS, :] = ones

    pz = z[:, 3 * aw:]
    prev = jnp.where(si == 0, 0.0, halo_ref[...])
    halo_ref[...] = pz[ts - POOL_HALO:, :]
    ext = jnp.concatenate([prev, pz], axis=0)
    tpos = si * ts + lax.broadcasted_iota(jnp.int32, (ts, 1), 0)
    outs = []
    for g, w in enumerate(POOL_WINDOWS):
        lo = g * LANES
        s = ext[:, lo:lo + LANES]
        span = 1
        while span < w:
            s = s[span:] + s[:-span]
            span *= 2
        s = s[POOL_HALO - (w - 1):POOL_HALO - (w - 1) + ts]
        cnt = jnp.minimum(tpos + 1, w).astype(F32)
        pooled = s / cnt - pz[:, lo:lo + LANES]
        outs.append(jnp.dot(pooled.astype(BF16), wp_ref[g], preferred_element_type=F32))
    po = jnp.concatenate(outs, axis=1) * ps_ref[...]
    pn_ref[0] = (_rms(po) * pg_ref[...]).astype(BF16)


def _in_call(x, sc, sh, g, w_in, cos_t, sa_t, sb_t, w_pool, pool_scale, pool_gain, ts=512):
    b, s, d = x.shape
    n_in = w_in.shape[1]
    pw = w_pool.shape[0] * w_pool.shape[1]
    aw = (n_in - pw) // 3
    nheads = aw // HEAD_DIM
    tok = lambda bi, si: (bi, si, 0)
    row = lambda bi, si: (bi, 0, 0)
    fix2 = lambda bi, si: (0, 0)
    return pl.pallas_call(
        functools.partial(_in_kernel, ts=ts, aw=aw),
        grid=(b, s // ts),
        in_specs=[pl.BlockSpec((1, ts, d), tok),
                  pl.BlockSpec((1, 1, d), row),
                  pl.BlockSpec((1, 1, d), row),
                  pl.BlockSpec((1, d), fix2),
                  pl.BlockSpec((d, n_in), fix2),
                  pl.BlockSpec((ts, LANES), lambda bi, si: (si, 0)),
                  pl.BlockSpec((ts, LANES), lambda bi, si: (si, 0)),
                  pl.BlockSpec((ts, LANES), lambda bi, si: (si, 0)),
                  pl.BlockSpec(w_pool.shape, lambda bi, si: (0, 0, 0)),
                  pl.BlockSpec((1, pw), fix2),
                  pl.BlockSpec((1, pw), fix2)],
        out_specs=[pl.BlockSpec((1, ts, aw), tok),
                   pl.BlockSpec((1, ts, aw), tok),
                   pl.BlockSpec((1, nheads, ts // PAIR_KEYS, VT_ROWS, PAIR_KEYS),
                                lambda bi, si: (bi, 0, si, 0, 0)),
                   pl.BlockSpec((1, ts, pw), tok)],
        out_shape=[jax.ShapeDtypeStruct((b, s, aw), BF16),
                   jax.ShapeDtypeStruct((b, s, aw), BF16),
                   jax.ShapeDtypeStruct((b, nheads, s // PAIR_KEYS, VT_ROWS, PAIR_KEYS), BF16),
                   jax.ShapeDtypeStruct((b, s, pw), BF16)],
        scratch_shapes=[pltpu.VMEM((POOL_HALO, pw), F32)],
        compiler_params=_params("arbitrary", "arbitrary"),
        name="in_proj",
    )(x, sc, sh, g, w_in, cos_t, sa_t, sb_t, w_pool, pool_scale, pool_gain)


def _attn_kernel(q_ref, k_ref, vt_ref, o_ref, km_ref, bias_ref, s0_ref, s1_ref, *, nb):
    s_slots = (s0_ref, s1_ref)
    t = pl.program_id(2)
    blk = MOBA_BLOCK
    tq = PAIR_KEYS
    heads = range(HEADS_PER_GROUP)

    @pl.when(t == 0)
    def _():
        for j in range(nb):
            kj = k_ref[0, j * blk:(j + 1) * blk, :].astype(F32)
            km_ref[j:j + 1, :] = jnp.mean(kj, axis=0, keepdims=True)

    q2 = q_ref[0]
    lane = lax.broadcasted_iota(jnp.int32, (1, LANES), 1)
    qh = [jnp.where((lane >= hh * HEAD_DIM) & (lane < (hh + 1) * HEAD_DIM), q2, jnp.zeros_like(q2))
          for hh in heads]

    def pair_keys(pi):
        return k_ref[0, pl.ds(pl.multiple_of(pi * PAIR_KEYS, PAIR_KEYS), PAIR_KEYS), :]

    def issue_scores(pi, slot):
        kp = pair_keys(pi)
        for hh in heads:
            s_slots[slot][hh] = lax.dot_general(kp, qh[hh], _NT, preferred_element_type=F32)

    def softmax_pv(hh, s, pi, state):
        b0 = bias_ref[hh, 2 * pi]
        b1 = bias_ref[hh, 2 * pi + 1]
        s0 = s[0:blk]
        s1 = s[blk:2 * blk]
        m_new = jnp.maximum(jnp.max(s0, axis=0, keepdims=True) + b0,
                            jnp.max(s1, axis=0, keepdims=True) + b1)
        if state is not None:
            m_new = jnp.maximum(state[0], m_new)
        p = jnp.concatenate([jnp.exp2(s0 - (m_new - b0)).astype(BF16),
                             jnp.exp2(s1 - (m_new - b1)).astype(BF16)], axis=0)
        acc = jnp.dot(vt_ref[0, hh, pi], p, preferred_element_type=F32)
        if state is not None:
            acc = jnp.exp2(state[0] - m_new) * state[1] + acc
        return m_new, acc

    km = km_ref[...]
    km_hi = km.astype(BF16)
    km_lo = (km - km_hi.astype(F32)).astype(BF16)
    lhs = jnp.concatenate([km_hi, km_lo, pair_keys(t)], axis=0)
    r = [lax.dot_general(lhs, qh[hh], _NT, preferred_element_type=F32) for hh in heads]
    issue_scores(0, 0)

    jidx = lax.broadcasted_iota(jnp.int32, (nb, 1), 0)
    qblk = 2 * t + lax.broadcasted_iota(jnp.int32, (1, tq), 1) // blk
    past = jidx < qblk
    for hh in heads:
        gate = jnp.where(past, r[hh][0:nb] + r[hh][nb:2 * nb], NEG_INF)
        rank = jnp.zeros(gate.shape, jnp.int32)
        for i in range(nb):
            gi = gate[i:i + 1, :]
            beats = (gi > gate) | ((gi == gate) & (i < jidx))
            rank = rank + beats.astype(jnp.int32)
        allowed = ((rank < MOBA_TOP_K) & past) | (jidx == qblk)
        bias = jnp.where(allowed, 0.0, NEG_INF).astype(F32)
        for j in range(nb):
            bias_ref[hh, j] = bias[j:j + 1, :]

    causal = (lax.broadcasted_iota(jnp.int32, (tq, tq), 0)
              <= lax.broadcasted_iota(jnp.int32, (tq, tq), 1))
    state = tuple(softmax_pv(hh, jnp.where(causal, r[hh][2 * nb:], NEG_INF), t, None)
                  for hh in heads)

    def trip(i, state, slot, prefetch=True):
        if prefetch:
            issue_scores(jnp.minimum(i + 1, nb // 2 - 1), 1 - slot)
        return tuple(softmax_pv(hh, s_slots[slot][hh], i, state[hh]) for hh in heads)

    def two_trips(u, state):
        return trip(2 * u + 1, trip(2 * u, state, 0), 1)

    state = lax.fori_loop(0, t // 2, two_trips, state)
    state = lax.cond(t % 2 == 1, lambda: trip(t - 1, state, 0, prefetch=False), lambda: state)
    outs = [acc[0:HEAD_DIM, :] / acc[HEAD_DIM:HEAD_DIM + 1, :] for _, acc in state]
    o_ref[0] = jnp.concatenate(outs, axis=0).T


def _attn_call(q, k, vt):
    b, s, aw = q.shape
    ngroups = aw // LANES
    nb = s // MOBA_BLOCK
    tq = PAIR_KEYS
    return pl.pallas_call(
        functools.partial(_attn_kernel, nb=nb),
        grid=(b, ngroups, s // tq),
        in_specs=[pl.BlockSpec((1, tq, LANES), lambda bi, gi, t: (bi, t, gi)),
                  pl.BlockSpec((1, s, LANES), lambda bi, gi, t: (bi, 0, gi)),
                  pl.BlockSpec((1, HEADS_PER_GROUP, nb // 2, VT_ROWS, PAIR_KEYS),
                               lambda bi, gi, t: (bi, gi, 0, 0, 0))],
        out_specs=pl.BlockSpec((1, tq, LANES), lambda bi, gi, t: (bi, t, gi)),
        out_shape=jax.ShapeDtypeStruct((b, s, aw), F32),
        scratch_shapes=[pltpu.VMEM((nb, LANES), F32),
                        pltpu.VMEM((HEADS_PER_GROUP, nb, 1, tq), F32),
                        pltpu.VMEM((HEADS_PER_GROUP, PAIR_KEYS, tq), F32),
                        pltpu.VMEM((HEADS_PER_GROUP, PAIR_KEYS, tq), F32)],
        compiler_params=_params("arbitrary", "arbitrary", "arbitrary"),
        name="moba_attn",
    )(q, k, vt)


def _ffn_kernel(ao_ref, pn_ref, xin_ref, ga_ref, wo_ref, gpm_ref, gt1_ref,
                sc_ref, sh_ref, g_ref, wu_ref, cw_ref, cb_ref, wd_ref, gp_ref, gt_ref,
                o_ref, tail_ref, act_ref, *, ts, tf):
    si = pl.program_id(1)
    aw = ao_ref.shape[2]
    an = (_rms(ao_ref[0]) * ga_ref[...]).astype(BF16)
    y = (jnp.dot(an, wo_ref[0:aw, :], preferred_element_type=F32)
         + jnp.dot(pn_ref[0], wo_ref[aw:, :], preferred_element_type=F32))
    x = xin_ref[0] + gt1_ref[0] * (_rms(y) * gpm_ref[...])

    hb = ((_rms(x) * g_ref[...]) * (1.0 + sc_ref[0]) + sh_ref[0]).astype(BF16)
    d_ff = wd_ref.shape[0]
    first = si == 0

    def conv(lo):
        u = jnp.dot(hb, wu_ref[:, lo:lo + tf], preferred_element_type=F32)
        tail = jnp.where(first, 0.0, tail_ref[:, lo:lo + tf])
        tail_ref[:, lo:lo + tf] = u[ts - CONV_HALO:, :]
        ucat = jnp.concatenate([tail, u], axis=0)
        uc = cb_ref[:, lo:lo + tf] + cw_ref[CONV_WIDTH - 1:CONV_WIDTH, lo:lo + tf] * u
        for j in range(CONV_WIDTH - 1):
            back = CONV_WIDTH - 1 - j
            uc = uc + cw_ref[j:j + 1, lo:lo + tf] * ucat[CONV_HALO - back:CONV_HALO - back + ts, :]
        return uc

    for fi in range(d_ff // tf):
        a = conv(fi * tf)
        g = conv(d_ff + fi * tf)
        act_ref[:, fi * tf:(fi + 1) * tf] = (a * jax.nn.sigmoid(a) * g).astype(BF16)
    y = jnp.dot(act_ref[...], wd_ref[...], preferred_element_type=F32)
    o_ref[0] = x + gt_ref[0] * (_rms(y) * gp_ref[...])


def _ffn_call(ao, pn, x, attn_gain, w_out, g_post_mix, gt1,
              sc, sh, g_pre, w_up, conv_w, conv_b, w_down, g_post, gt, ts=512, tf=256):
    b, s, d = x.shape
    f = w_down.shape[0]
    aw = ao.shape[2]
    pw = pn.shape[2]
    tok = lambda bi, si: (bi, si, 0)
    row = lambda bi, si: (bi, 0, 0)
    fix2 = lambda bi, si: (0, 0)
    resident = dict(pipeline_mode=pl.Buffered(1))
    return pl.pallas_call(
        functools.partial(_ffn_kernel, ts=ts, tf=tf),
        grid=(b, s // ts),
        in_specs=[pl.BlockSpec((1, ts, aw), tok),
                  pl.BlockSpec((1, ts, pw), tok),
                  pl.BlockSpec((1, ts, d), tok),
                  pl.BlockSpec((1, aw), fix2),
                  pl.BlockSpec(w_out.shape, fix2, **resident),
                  pl.BlockSpec((1, d), fix2),
                  pl.BlockSpec((1, 1, d), row),
                  pl.BlockSpec((1, 1, d), row),
                  pl.BlockSpec((1, 1, d), row),
                  pl.BlockSpec((1, d), fix2),
                  pl.BlockSpec(w_up.shape, fix2, **resident),
                  pl.BlockSpec(conv_w.shape, fix2),
                  pl.BlockSpec((1, 2 * f), fix2),
                  pl.BlockSpec(w_down.shape, fix2, **resident),
                  pl.BlockSpec((1, d), fix2),
                  pl.BlockSpec((1, 1, d), row)],
        out_specs=pl.BlockSpec((1, ts, d), tok),
        out_shape=jax.ShapeDtypeStruct((b, s, d), F32),
        scratch_shapes=[pltpu.VMEM((CONV_HALO, 2 * f), F32),
                        pltpu.VMEM((ts, f), BF16)],
        compiler_params=_params("arbitrary", "arbitrary"),
        name="conv_ffn",
    )(ao, pn, x, attn_gain, w_out, g_post_mix, gt1,
      sc, sh, g_pre, w_up, conv_w, conv_b[None], w_down, g_post, gt)


def _rotary_tables(s):
    half = ROT_DIM // 2
    pos = jnp.arange(s, dtype=F32)
    inv_freq = jnp.power(jnp.float32(ROPE_THETA), -jnp.arange(0, ROT_DIM, 2, dtype=F32) / ROT_DIM)
    ang = pos[:, None] * inv_freq[None, :]
    cos = jnp.cos(ang)
    sin = jnp.sin(ang)
    ones = jnp.ones((s, HEAD_DIM - ROT_DIM), F32)
    zeros = jnp.zeros((s, HEAD_DIM - ROT_DIM), F32)
    zh = jnp.zeros((s, half), F32)
    cos_h = jnp.concatenate([cos, cos, ones], axis=1)
    sa_h = jnp.concatenate([-sin, zh, zeros], axis=1)
    sb_h = jnp.concatenate([zh, sin, zeros], axis=1)
    rep = lambda t: jnp.concatenate([t] * HEADS_PER_GROUP, axis=1)
    return rep(cos_h), rep(sa_h), rep(sb_h)


def kernel(x, c, w_ada, b_ada, g_pre_mix, w_in, w_pool, pool_scale, attn_out_gain, pool_out_gain,
           w_out, g_post_mix, g_pre_ffn, w_up, conv_w, conv_b, w_down, g_post_ffn):
    b, s, d = x.shape
    depth = w_ada.shape[0]
    assert s % PAIR_KEYS == 0 and d % LANES == 0
    mod = _mod_call(c, w_ada, b_ada)
    cos_t, sa_t, sb_t = _rotary_tables(s)
    for l in range(depth):
        sh1, sc1, gt1, sh2, sc2, gt2 = (mod[l, :, i * d:(i + 1) * d].reshape(b, 1, d)
                                        for i in range(6))
        q, k, vt, pn = _in_call(x, sc1, sh1, g_pre_mix[l][None], w_in[l].astype(BF16),
                                cos_t, sa_t, sb_t, w_pool[l].astype(BF16),
                                pool_scale[l][None], pool_out_gain[l][None])
        ao = _attn_call(q, k, vt)
        x = _ffn_call(ao, pn, x, attn_out_gain[l][None], w_out[l].astype(BF16),
                      g_post_mix[l][None], gt1,
                      sc2, sh2, g_pre_ffn[l][None], w_up[l].astype(BF16), conv_w[l], conv_b[l],
                      w_down[l].astype(BF16), g_post_ffn[l][None], gt2)
    return x
```

```python
import functools

import jax
import jax.numpy as jnp
import numpy as np
from jax import lax
from jax.experimental import pallas as pl
from jax.experimental.pallas import tpu as pltpu

F32 = jnp.float32
BF16 = jnp.bfloat16

HEAD_DIM = 64
HEADS_PER_GROUP = 2
LANES = 128
MOBA_BLOCK = 256
MOBA_TOP_K = 3
POOL_WINDOWS = (2, 4, 8, 16)
POOL_HALO = 16
ROT_DIM = HEAD_DIM // 4
ROPE_THETA = 500000.0
CONV_WIDTH = 3
CONV_HALO = 8
NORM_EPS = 1e-6
NEG_INF = -1e30
LOG2_E = 1.4426950408889634
VT_ROWS = HEAD_DIM + 16
PAIR_KEYS = 2 * MOBA_BLOCK
VMEM_LIMIT = 48 * 1024 * 1024
MOD_SH1, MOD_SC1, MOD_GT1, MOD_SH2, MOD_SC2, MOD_GT2 = range(6)

_NT = (((1,), (1,)), ((), ()))


def _rms(x):
    return x * lax.rsqrt(jnp.mean(x * x, axis=-1, keepdims=True) + NORM_EPS)


def _params(*sem):
    return pltpu.CompilerParams(dimension_semantics=sem, vmem_limit_bytes=VMEM_LIMIT)


def _mod_kernel(c_ref, w_ref, b_ref, o_ref):
    c = c_ref[...]
    c_act = c * jax.nn.sigmoid(c)
    o_ref[0] = jnp.dot(c_act.astype(BF16), w_ref[0].astype(BF16),
                       preferred_element_type=F32) + b_ref[0]


def _mod_call(c, w_ada, b_ada, tn=1536):
    nl, d, n6 = w_ada.shape
    b = c.shape[0]
    return pl.pallas_call(
        _mod_kernel,
        grid=(nl, n6 // tn),
        in_specs=[pl.BlockSpec((b, d), lambda l, j: (0, 0)),
                  pl.BlockSpec((1, d, tn), lambda l, j: (l, 0, j)),
                  pl.BlockSpec((1, 1, tn), lambda l, j: (l, 0, j))],
        out_specs=pl.BlockSpec((1, b, tn), lambda l, j: (l, 0, j)),
        out_shape=jax.ShapeDtypeStruct((nl, b, n6), F32),
        compiler_params=_params("arbitrary", "arbitrary"),
        name="adaln_mod",
    )(c, w_ada, b_ada.reshape(nl, 1, n6))


def _in_kernel(x_ref, sc_ref, sh_ref, g_ref, w_ref, cos_ref, sa_ref, sb_ref, wp_ref, ps_ref,
               pg_ref, q_ref, k_ref, vt_ref, pn_ref, halo_ref, *, ts, aw):
    si = pl.program_id(1)
    h = _rms(x_ref[0]) * (g_ref[...] * (1.0 + sc_ref[0])) + sh_ref[0]
    z = jnp.dot(h.astype(BF16), w_ref[...], preferred_element_type=F32)

    cos = cos_ref[...]
    sa = sa_ref[...]
    sb = sb_ref[...]

    def rot(t):
        return t * cos + pltpu.roll(t, LANES - ROT_DIM // 2, 1) * sa + pltpu.roll(t, ROT_DIM // 2, 1) * sb

    scale = HEAD_DIM ** -0.5 * LOG2_E
    ngroups = aw // LANES
    ones = jnp.ones((VT_ROWS - HEAD_DIM, PAIR_KEYS), BF16)
    for gi in range(ngroups):
        lo = gi * LANES
        q_ref[0, :, lo:lo + LANES] = (rot(z[:, lo:lo + LANES]) * scale).astype(BF16)
        k_ref[0, :, lo:lo + LANES] = rot(z[:, aw + lo:aw + lo + LANES]).astype(BF16)
        vt = z[:, 2 * aw + lo:2 * aw + lo + LANES].T.astype(BF16)
        for hh in range(HEADS_PER_GROUP):
            head = gi * HEADS_PER_GROUP + hh
            for jp in range(ts // PAIR_KEYS):
                vt_ref[0, head, jp, 0:HEAD_DIM, :] = vt[hh * HEAD_DIM:(hh + 1) * HEAD_DIM,
                                                        jp * PAIR_KEYS:(jp + 1) * PAIR_KEYS]
                vt_ref[0, head, jp, HEAD_DIM:VT_ROWS, :] = ones

    pz = z[:, 3 * aw:]
    prev = jnp.where(si == 0, 0.0, halo_ref[...])
    halo_ref[...] = pz[ts - POOL_HALO:, :]
    ext = jnp.concatenate([prev, pz], axis=0)
    tpos = si * ts + lax.broadcasted_iota(jnp.int32, (ts, 1), 0)
    outs = []
    for g, w in enumerate(POOL_WINDOWS):
        lo = g * LANES
        s = ext[:, lo:lo + LANES]
        span = 1
        while span < w:
            s = s[span:] + s[:-span]
            span *= 2
        s = s[POOL_HALO - (w - 1):POOL_HALO - (w - 1) + ts]
        cnt = jnp.minimum(tpos + 1, w).astype(F32)
        pooled = s / cnt - pz[:, lo:lo + LANES]
        outs.append(jnp.dot(pooled.astype(BF16), wp_ref[g], preferred_element_type=F32))
    po = jnp.concatenate(outs, axis=1) * ps_ref[...]
    pn_ref[0] = (_rms(po) * pg_ref[...]).astype(BF16)


def _mod_spec(l, chunk, d):
    return pl.BlockSpec((None, None, 1, 1, d), lambda bi, si: (l, bi, chunk, 0, 0))


def _layer_spec(l, arr, **kw):
    return pl.BlockSpec((None,) + arr.shape[1:], lambda bi, si: (l,) + (0,) * (arr.ndim - 1), **kw)


def _in_call(l, x, mod, g, w_in, cos_t, sa_t, sb_t, w_pool, pool_scale, pool_gain, ts=512):
    b, s, d = x.shape
    n_in = w_in.shape[2]
    pw = w_pool.shape[1] * w_pool.shape[2]
    aw = (n_in - pw) // 3
    nheads = aw // HEAD_DIM
    tok = lambda bi, si: (bi, si, 0)
    return pl.pallas_call(
        functools.partial(_in_kernel, ts=ts, aw=aw),
        grid=(b, s // ts),
        in_specs=[pl.BlockSpec((1, ts, d), tok),
                  _mod_spec(l, MOD_SC1, d),
                  _mod_spec(l, MOD_SH1, d),
                  _layer_spec(l, g),
                  _layer_spec(l, w_in),
                  pl.BlockSpec((ts, LANES), lambda bi, si: (si, 0)),
                  pl.BlockSpec((ts, LANES), lambda bi, si: (si, 0)),
                  pl.BlockSpec((ts, LANES), lambda bi, si: (si, 0)),
                  _layer_spec(l, w_pool),
                  _layer_spec(l, pool_scale),
                  _layer_spec(l, pool_gain)],
        out_specs=[pl.BlockSpec((1, ts, aw), tok),
                   pl.BlockSpec((1, ts, aw), tok),
                   pl.BlockSpec((1, nheads, ts // PAIR_KEYS, VT_ROWS, PAIR_KEYS),
                                lambda bi, si: (bi, 0, si, 0, 0)),
                   pl.BlockSpec((1, ts, pw), tok)],
        out_shape=[jax.ShapeDtypeStruct((b, s, aw), BF16),
                   jax.ShapeDtypeStruct((b, s, aw), BF16),
                   jax.ShapeDtypeStruct((b, nheads, s // PAIR_KEYS, VT_ROWS, PAIR_KEYS), BF16),
                   jax.ShapeDtypeStruct((b, s, pw), BF16)],
        scratch_shapes=[pltpu.VMEM((POOL_HALO, pw), F32)],
        compiler_params=_params("arbitrary", "arbitrary"),
        name="in_proj",
    )(x, mod, mod, g, w_in, cos_t, sa_t, sb_t, w_pool, pool_scale, pool_gain)


def _attn_kernel(q_ref, k_ref, vt_ref, o_ref, km_ref, bias_ref, s0_ref, s1_ref, *, nb):
    s_slots = (s0_ref, s1_ref)
    t = pl.program_id(2)
    blk = MOBA_BLOCK
    tq = PAIR_KEYS
    heads = range(HEADS_PER_GROUP)

    @pl.when(t == 0)
    def _():
        for j in range(nb):
            kj = k_ref[0, j * blk:(j + 1) * blk, :].astype(F32)
            km_ref[j:j + 1, :] = jnp.mean(kj, axis=0, keepdims=True)

    q2 = q_ref[0]
    lane = lax.broadcasted_iota(jnp.int32, (1, LANES), 1)
    qh = [jnp.where((lane >= hh * HEAD_DIM) & (lane < (hh + 1) * HEAD_DIM), q2, jnp.zeros_like(q2))
          for hh in heads]

    def pair_keys(pi):
        return k_ref[0, pl.ds(pl.multiple_of(pi * PAIR_KEYS, PAIR_KEYS), PAIR_KEYS), :]

    def block_max(s):
        return (jnp.max(s[0:blk], axis=0, keepdims=True), jnp.max(s[blk:2 * blk], axis=0, keepdims=True))

    def issue_scores(pi, slot):
        kp = pair_keys(pi)
        maxes = []
        for hh in heads:
            s = lax.dot_general(kp, qh[hh], _NT, preferred_element_type=F32)
            s_slots[slot][hh] = s
            maxes.append(block_max(s))
        return tuple(maxes)

    def softmax_pv(hh, s, cmax, pi, state):
        b0 = bias_ref[hh, 2 * pi]
        b1 = bias_ref[hh, 2 * pi + 1]
        s0 = s[0:blk]
        s1 = s[blk:2 * blk]
        m_new = jnp.maximum(cmax[0] + b0, cmax[1] + b1)
        if state is not None:
            m_new = jnp.maximum(state[0], m_new)
        p = jnp.concatenate([jnp.exp2(s0 - (m_new - b0)).astype(BF16),
                             jnp.exp2(s1 - (m_new - b1)).astype(BF16)], axis=0)
        acc = jnp.dot(vt_ref[0, hh, pi], p, preferred_element_type=F32)
        if state is not None:
            acc = jnp.exp2(state[0] - m_new) * state[1] + acc
        return m_new, acc

    km = km_ref[...]
    km_hi = km.astype(BF16)
    km_lo = (km - km_hi.astype(F32)).astype(BF16)
    lhs = jnp.concatenate([km_hi, km_lo, pair_keys(t)], axis=0)
    r = [lax.dot_general(lhs, qh[hh], _NT, preferred_element_type=F32) for hh in heads]
    maxes0 = issue_scores(0, 0)

    jidx = lax.broadcasted_iota(jnp.int32, (nb, 1), 0)
    qblk = 2 * t + lax.broadcasted_iota(jnp.int32, (1, tq), 1) // blk
    past = jidx < qblk
    for hh in heads:
        gate = jnp.where(past, r[hh][0:nb] + r[hh][nb:2 * nb], NEG_INF)
        rank = jnp.zeros(gate.shape, jnp.int32)
        for i in range(nb):
            gi = gate[i:i + 1, :]
            beats = (gi > gate) | ((gi == gate) & (i < jidx))
            rank = rank + beats.astype(jnp.int32)
        allowed = ((rank < MOBA_TOP_K) & past) | (jidx == qblk)
        bias = jnp.where(allowed, 0.0, NEG_INF).astype(F32)
        for j in range(nb):
            bias_ref[hh, j] = bias[j:j + 1, :]

    causal = (lax.broadcasted_iota(jnp.int32, (tq, tq), 0)
              <= lax.broadcasted_iota(jnp.int32, (tq, tq), 1))
    own = []
    for hh in heads:
        s = jnp.where(causal, r[hh][2 * nb:], NEG_INF)
        own.append(softmax_pv(hh, s, block_max(s), t, None))

    def trip(i, carry, slot, prefetch=True):
        state, maxes = carry
        nxt = issue_scores(jnp.minimum(i + 1, nb // 2 - 1), 1 - slot) if prefetch else maxes
        return tuple(softmax_pv(hh, s_slots[slot][hh], maxes[hh], i, state[hh]) for hh in heads), nxt

    def two_trips(u, carry):
        return trip(2 * u + 1, trip(2 * u, carry, 0), 1)

    carry = lax.fori_loop(0, t // 2, two_trips, (tuple(own), maxes0))
    carry = lax.cond(t % 2 == 1, lambda: trip(t - 1, carry, 0, prefetch=False), lambda: carry)
    outs = [acc[0:HEAD_DIM, :] / acc[HEAD_DIM:HEAD_DIM + 1, :] for _, acc in carry[0]]
    o_ref[0] = jnp.concatenate(outs, axis=0).T


def _attn_call(q, k, vt):
    b, s, aw = q.shape
    ngroups = aw // LANES
    nb = s // MOBA_BLOCK
    tq = PAIR_KEYS
    return pl.pallas_call(
        functools.partial(_attn_kernel, nb=nb),
        grid=(b, ngroups, s // tq),
        in_specs=[pl.BlockSpec((1, tq, LANES), lambda bi, gi, t: (bi, t, gi)),
                  pl.BlockSpec((1, s, LANES), lambda bi, gi, t: (bi, 0, gi)),
                  pl.BlockSpec((1, HEADS_PER_GROUP, nb // 2, VT_ROWS, PAIR_KEYS),
                               lambda bi, gi, t: (bi, gi, 0, 0, 0))],
        out_specs=pl.BlockSpec((1, tq, LANES), lambda bi, gi, t: (bi, t, gi)),
        out_shape=jax.ShapeDtypeStruct((b, s, aw), F32),
        scratch_shapes=[pltpu.VMEM((nb, LANES), F32),
                        pltpu.VMEM((HEADS_PER_GROUP, nb, 1, tq), F32),
                        pltpu.VMEM((HEADS_PER_GROUP, PAIR_KEYS, tq), F32),
                        pltpu.VMEM((HEADS_PER_GROUP, PAIR_KEYS, tq), F32)],
        compiler_params=_params("arbitrary", "arbitrary", "arbitrary"),
        name="moba_attn",
    )(q, k, vt)


def _ffn_kernel(ao_ref, pn_ref, xin_ref, ga_ref, wo_ref, gpm_ref, gt1_ref,
                sc_ref, sh_ref, g_ref, wu_ref, cw_ref, cb_ref, wd_ref, gp_ref, gt_ref,
                o_ref, tail_ref, act_ref, *, ts, tf):
    si = pl.program_id(1)
    aw = ao_ref.shape[2]
    an = (_rms(ao_ref[0]) * ga_ref[...]).astype(BF16)
    y = (jnp.dot(an, wo_ref[0:aw, :], preferred_element_type=F32)
         + jnp.dot(pn_ref[0], wo_ref[aw:, :], preferred_element_type=F32))
    x = xin_ref[0] + _rms(y) * (gt1_ref[0] * gpm_ref[...])

    hb = (_rms(x) * (g_ref[...] * (1.0 + sc_ref[0])) + sh_ref[0]).astype(BF16)
    d_ff = wd_ref.shape[0]
    first = si == 0

    def conv(lo):
        u = jnp.dot(hb, wu_ref[:, lo:lo + tf], preferred_element_type=F32)
        tail = jnp.where(first, 0.0, tail_ref[:, lo:lo + tf])
        tail_ref[:, lo:lo + tf] = u[ts - CONV_HALO:, :]
        ucat = jnp.concatenate([tail, u], axis=0)
        uc = cb_ref[:, lo:lo + tf] + cw_ref[CONV_WIDTH - 1:CONV_WIDTH, lo:lo + tf] * u
        for j in range(CONV_WIDTH - 1):
            back = CONV_WIDTH - 1 - j
            uc = uc + cw_ref[j:j + 1, lo:lo + tf] * ucat[CONV_HALO - back:CONV_HALO - back + ts, :]
        return uc

    for fi in range(d_ff // tf):
        a = conv(fi * tf)
        g = conv(d_ff + fi * tf)
        act_ref[:, fi * tf:(fi + 1) * tf] = (a * jax.nn.sigmoid(a) * g).astype(BF16)
    y = jnp.dot(act_ref[...], wd_ref[...], preferred_element_type=F32)
    o_ref[0] = x + _rms(y) * (gt_ref[0] * gp_ref[...])


def _ffn_call(l, ao, pn, x, mod, attn_gain, w_out, g_post_mix,
              g_pre, w_up, conv_w, conv_b, w_down, g_post, ts=512, tf=256):
    b, s, d = x.shape
    f = w_down.shape[1]
    aw = ao.shape[2]
    pw = pn.shape[2]
    tok = lambda bi, si: (bi, si, 0)
    resident = dict(pipeline_mode=pl.Buffered(1))
    return pl.pallas_call(
        functools.partial(_ffn_kernel, ts=ts, tf=tf),
        grid=(b, s // ts),
        in_specs=[pl.BlockSpec((1, ts, aw), tok),
                  pl.BlockSpec((1, ts, pw), tok),
                  pl.BlockSpec((1, ts, d), tok),
                  _layer_spec(l, attn_gain),
                  _layer_spec(l, w_out, **resident),
                  _layer_spec(l, g_post_mix),
                  _mod_spec(l, MOD_GT1, d),
                  _mod_spec(l, MOD_SC2, d),
                  _mod_spec(l, MOD_SH2, d),
                  _layer_spec(l, g_pre),
                  _layer_spec(l, w_up, **resident),
                  _layer_spec(l, conv_w),
                  _layer_spec(l, conv_b),
                  _layer_spec(l, w_down, **resident),
                  _layer_spec(l, g_post),
                  _mod_spec(l, MOD_GT2, d)],
        out_specs=pl.BlockSpec((1, ts, d), tok),
        out_shape=jax.ShapeDtypeStruct((b, s, d), F32),
        scratch_shapes=[pltpu.VMEM((CONV_HALO, 2 * f), F32),
                        pltpu.VMEM((ts, f), BF16)],
        compiler_params=_params("arbitrary", "arbitrary"),
        name="conv_ffn",
    )(ao, pn, x, attn_gain, w_out, g_post_mix, mod,
      mod, mod, g_pre, w_up, conv_w, conv_b, w_down, g_post, mod)


def _rotary_tables(s):
    half = ROT_DIM // 2
    pos = jnp.arange(s, dtype=F32)
    inv_freq = jnp.power(jnp.float32(ROPE_THETA), -jnp.arange(0, ROT_DIM, 2, dtype=F32) / ROT_DIM)
    dim = np.arange(LANES) % HEAD_DIM
    freq = jnp.where(dim < ROT_DIM, inv_freq[dim % half], 0.0)
    ang = pos[:, None] * freq[None, :]
    cos = jnp.cos(ang)
    sin = jnp.sin(ang)
    sa = jnp.where(dim < half, -sin, 0.0)
    sb = jnp.where((dim >= half) & (dim < ROT_DIM), sin, 0.0)
    return cos, sa, sb


def kernel(x, c, w_ada, b_ada, g_pre_mix, w_in, w_pool, pool_scale, attn_out_gain, pool_out_gain,
           w_out, g_post_mix, g_pre_ffn, w_up, conv_w, conv_b, w_down, g_post_ffn):
    b, s, d = x.shape
    depth = w_ada.shape[0]
    assert s % PAIR_KEYS == 0 and d % LANES == 0
    mod = _mod_call(c, w_ada, b_ada).reshape(depth, b, 6, 1, d)
    cos_t, sa_t, sb_t = _rotary_tables(s)
    w_in, w_pool, w_out, w_up, w_down = (w.astype(BF16) for w in (w_in, w_pool, w_out, w_up, w_down))
    (g_pre_mix, pool_scale, pool_out_gain, attn_out_gain, g_post_mix, g_pre_ffn, conv_b,
     g_post_ffn) = (v[:, None, :] for v in (g_pre_mix, pool_scale, pool_out_gain, attn_out_gain,
                                            g_post_mix, g_pre_ffn, conv_b, g_post_ffn))
    for l in range(depth):
        q, k, vt, pn = _in_call(l, x, mod, g_pre_mix, w_in, cos_t, sa_t, sb_t, w_pool,
                                pool_scale, pool_out_gain)
        ao = _attn_call(q, k, vt)
        x = _ffn_call(l, ao, pn, x, mod, attn_out_gain, w_out, g_post_mix,
                      g_pre_ffn, w_up, conv_w, conv_b, w_down, g_post_ffn)
    return x
```

```python
import functools

import jax
import jax.numpy as jnp
import numpy as np
from jax import lax
from jax.experimental import pallas as pl
from jax.experimental.pallas import tpu as pltpu

F32 = jnp.float32
BF16 = jnp.bfloat16

HEAD_DIM = 64
HEADS_PER_GROUP = 2
LANES = 128
MOBA_BLOCK = 256
MOBA_TOP_K = 3
POOL_WINDOWS = (2, 4, 8, 16)
POOL_HALO = 16
ROT_DIM = HEAD_DIM // 4
ROPE_THETA = 500000.0
CONV_WIDTH = 3
CONV_HALO = 8
NORM_EPS = 1e-6
NEG_INF = -1e30
LOG2_E = 1.4426950408889634
VT_ROWS = HEAD_DIM + 16
PAIR_KEYS = 2 * MOBA_BLOCK
VMEM_LIMIT = 48 * 1024 * 1024
MOD_SH1, MOD_SC1, MOD_GT1, MOD_SH2, MOD_SC2, MOD_GT2 = range(6)

_NT = (((1,), (1,)), ((), ()))


def _rms(x):
    return x * lax.rsqrt(jnp.mean(x * x, axis=-1, keepdims=True) + NORM_EPS)


def _params(*sem):
    return pltpu.CompilerParams(dimension_semantics=sem, vmem_limit_bytes=VMEM_LIMIT)


def _mod_kernel(c_ref, w_ref, b_ref, o_ref):
    c = c_ref[...]
    c_act = c * jax.nn.sigmoid(c)
    o_ref[0] = jnp.dot(c_act.astype(BF16), w_ref[0].astype(BF16),
                       preferred_element_type=F32) + b_ref[0]


def _mod_call(c, w_ada, b_ada, tn=1536):
    nl, d, n6 = w_ada.shape
    b = c.shape[0]
    return pl.pallas_call(
        _mod_kernel,
        grid=(nl, n6 // tn),
        in_specs=[pl.BlockSpec((b, d), lambda l, j: (0, 0)),
                  pl.BlockSpec((1, d, tn), lambda l, j: (l, 0, j)),
                  pl.BlockSpec((1, 1, tn), lambda l, j: (l, 0, j))],
        out_specs=pl.BlockSpec((1, b, tn), lambda l, j: (l, 0, j)),
        out_shape=jax.ShapeDtypeStruct((nl, b, n6), F32),
        compiler_params=_params("arbitrary", "arbitrary"),
        name="adaln_mod",
    )(c, w_ada, b_ada.reshape(nl, 1, n6))


def _in_kernel(x_ref, sc_ref, sh_ref, g_ref, w_ref, cos_ref, sa_ref, sb_ref, wp_ref, ps_ref,
               pg_ref, q_ref, k_ref, vt_ref, pn_ref, halo_ref, *, ts, aw):
    si = pl.program_id(1)
    h = _rms(x_ref[0]) * (g_ref[...] * (1.0 + sc_ref[0])) + sh_ref[0]
    z = jnp.dot(h.astype(BF16), w_ref[...], preferred_element_type=F32)

    cos = cos_ref[...]
    sa = sa_ref[...]
    sb = sb_ref[...]

    def rot(t):
        return t * cos + pltpu.roll(t, LANES - ROT_DIM // 2, 1) * sa + pltpu.roll(t, ROT_DIM // 2, 1) * sb

    scale = HEAD_DIM ** -0.5 * LOG2_E
    ngroups = aw // LANES
    ones = jnp.ones((VT_ROWS - HEAD_DIM, PAIR_KEYS), BF16)
    for gi in range(ngroups):
        lo = gi * LANES
        q_ref[0, :, lo:lo + LANES] = (rot(z[:, lo:lo + LANES]) * scale).astype(BF16)
        k_ref[0, :, lo:lo + LANES] = rot(z[:, aw + lo:aw + lo + LANES]).astype(BF16)
        vt = z[:, 2 * aw + lo:2 * aw + lo + LANES].T.astype(BF16)
        for hh in range(HEADS_PER_GROUP):
            head = gi * HEADS_PER_GROUP + hh
            for jp in range(ts // PAIR_KEYS):
                vt_ref[0, head, jp, 0:HEAD_DIM, :] = vt[hh * HEAD_DIM:(hh + 1) * HEAD_DIM,
                                                        jp * PAIR_KEYS:(jp + 1) * PAIR_KEYS]
                vt_ref[0, head, jp, HEAD_DIM:VT_ROWS, :] = ones

    pz = z[:, 3 * aw:]
    prev = jnp.where(si == 0, 0.0, halo_ref[...])
    halo_ref[...] = pz[ts - POOL_HALO:, :]
    ext = jnp.concatenate([prev, pz], axis=0)
    tpos = si * ts + lax.broadcasted_iota(jnp.int32, (ts, 1), 0)
    outs = []
    for g, w in enumerate(POOL_WINDOWS):
        lo = g * LANES
        s = ext[:, lo:lo + LANES]
        span = 1
        while span < w:
            s = s[span:] + s[:-span]
            span *= 2
        s = s[POOL_HALO - (w - 1):POOL_HALO - (w - 1) + ts]
        cnt = jnp.minimum(tpos + 1, w).astype(F32)
        pooled = s / cnt - pz[:, lo:lo + LANES]
        outs.append(jnp.dot(pooled.astype(BF16), wp_ref[g], preferred_element_type=F32))
    po = jnp.concatenate(outs, axis=1) * ps_ref[...]
    pn_ref[0] = (_rms(po) * pg_ref[...]).astype(BF16)


def _mod_spec(l, chunk, d):
    return pl.BlockSpec((None, None, 1, 1, d), lambda bi, si: (l, bi, chunk, 0, 0))


def _layer_spec(l, arr, **kw):
    return pl.BlockSpec((None,) + arr.shape[1:], lambda bi, si: (l,) + (0,) * (arr.ndim - 1), **kw)


def _in_call(l, x, mod, g, w_in, cos_t, sa_t, sb_t, w_pool, pool_scale, pool_gain, ts=512):
    b, s, d = x.shape
    n_in = w_in.shape[2]
    pw = w_pool.shape[1] * w_pool.shape[2]
    aw = (n_in - pw) // 3
    nheads = aw // HEAD_DIM
    tok = lambda bi, si: (bi, si, 0)
    return pl.pallas_call(
        functools.partial(_in_kernel, ts=ts, aw=aw),
        grid=(b, s // ts),
        in_specs=[pl.BlockSpec((1, ts, d), tok),
                  _mod_spec(l, MOD_SC1, d),
                  _mod_spec(l, MOD_SH1, d),
                  _layer_spec(l, g),
                  _layer_spec(l, w_in),
                  pl.BlockSpec((ts, LANES), lambda bi, si: (si, 0)),
                  pl.BlockSpec((ts, LANES), lambda bi, si: (si, 0)),
                  pl.BlockSpec((ts, LANES), lambda bi, si: (si, 0)),
                  _layer_spec(l, w_pool),
                  _layer_spec(l, pool_scale),
                  _layer_spec(l, pool_gain)],
        out_specs=[pl.BlockSpec((1, ts, aw), tok),
                   pl.BlockSpec((1, ts, aw), tok),
                   pl.BlockSpec((1, nheads, ts // PAIR_KEYS, VT_ROWS, PAIR_KEYS),
                                lambda bi, si: (bi, 0, si, 0, 0)),
                   pl.BlockSpec((1, ts, pw), tok)],
        out_shape=[jax.ShapeDtypeStruct((b, s, aw), BF16),
                   jax.ShapeDtypeStruct((b, s, aw), BF16),
                   jax.ShapeDtypeStruct((b, nheads, s // PAIR_KEYS, VT_ROWS, PAIR_KEYS), BF16),
                   jax.ShapeDtypeStruct((b, s, pw), BF16)],
        scratch_shapes=[pltpu.VMEM((POOL_HALO, pw), F32)],
        compiler_params=_params("arbitrary", "arbitrary"),
        name="in_proj",
    )(x, mod, mod, g, w_in, cos_t, sa_t, sb_t, w_pool, pool_scale, pool_gain)


def _attn_kernel(q_ref, k_ref, vt_ref, o_ref, km_ref, bias_ref, s0_ref, s1_ref, *, nb):
    s_slots = (s0_ref, s1_ref)
    t = pl.program_id(2)
    blk = MOBA_BLOCK
    tq = PAIR_KEYS
    heads = range(HEADS_PER_GROUP)

    @pl.when(t == 0)
    def _():
        for j in range(nb):
            kj = k_ref[0, j * blk:(j + 1) * blk, :].astype(F32)
            km_ref[j:j + 1, :] = jnp.mean(kj, axis=0, keepdims=True)

    q2 = q_ref[0]
    lane = lax.broadcasted_iota(jnp.int32, (1, LANES), 1)
    qh = [jnp.where((lane >= hh * HEAD_DIM) & (lane < (hh + 1) * HEAD_DIM), q2, jnp.zeros_like(q2))
          for hh in heads]

    def pair_keys(pi):
        return k_ref[0, pl.ds(pl.multiple_of(pi * PAIR_KEYS, PAIR_KEYS), PAIR_KEYS), :]

    def block_max(s):
        return (jnp.max(s[0:blk], axis=0, keepdims=True), jnp.max(s[blk:2 * blk], axis=0, keepdims=True))

    def issue_scores(pi, slot):
        kp = pair_keys(pi)
        maxes = []
        for hh in heads:
            s = lax.dot_general(kp, qh[hh], _NT, preferred_element_type=F32)
            s_slots[slot][hh] = s
            maxes.append(block_max(s))
        return tuple(maxes)

    def softmax_pv(hh, s, cmax, pi, state):
        b0 = bias_ref[hh, 2 * pi]
        b1 = bias_ref[hh, 2 * pi + 1]
        s0 = s[0:blk]
        s1 = s[blk:2 * blk]
        m_new = jnp.maximum(cmax[0] + b0, cmax[1] + b1)
        if state is not None:
            m_new = jnp.maximum(state[0], m_new)
        p = jnp.concatenate([jnp.exp2(s0 - (m_new - b0)).astype(BF16),
                             jnp.exp2(s1 - (m_new - b1)).astype(BF16)], axis=0)
        acc = jnp.dot(vt_ref[0, hh, pi], p, preferred_element_type=F32)
        if state is not None:
            acc = jnp.exp2(state[0] - m_new) * state[1] + acc
        return m_new, acc

    km = km_ref[...]
    km_hi = km.astype(BF16)
    km_lo = (km - km_hi.astype(F32)).astype(BF16)
    kown = pair_keys(t)
    lhs = jnp.concatenate([km_hi, km_lo, kown[0:blk]], axis=0)
    r = [lax.dot_general(lhs, qh[hh], _NT, preferred_element_type=F32) for hh in heads]
    r2 = [lax.dot_general(kown[blk:], qh[hh][blk:], _NT, preferred_element_type=F32) for hh in heads]
    maxes0 = issue_scores(0, 0)

    jidx = lax.broadcasted_iota(jnp.int32, (nb, 1), 0)
    qblk = 2 * t + lax.broadcasted_iota(jnp.int32, (1, tq), 1) // blk
    past = jidx < qblk
    for hh in heads:
        gate = jnp.where(past, r[hh][0:nb] + r[hh][nb:2 * nb], NEG_INF)
        picked = jnp.zeros(gate.shape, jnp.bool_)
        for _ in range(min(MOBA_TOP_K, nb)):
            top = jnp.max(gate, axis=0, keepdims=True)
            first = jnp.min(jnp.where(gate == top, jidx, nb), axis=0, keepdims=True)
            hit = jidx == first
            picked = picked | hit
            gate = jnp.where(hit, -jnp.inf, gate)
        bias = jnp.where(picked & past, 0.0, NEG_INF).astype(F32)
        for j in range(nb):
            bias_ref[hh, j] = bias[j:j + 1, :]

    causal = (lax.broadcasted_iota(jnp.int32, (blk, blk), 0)
              <= lax.broadcasted_iota(jnp.int32, (blk, blk), 1))
    own = []
    for hh in heads:
        vt_own = vt_ref[0, hh, t]
        s_a = jnp.where(causal, r[hh][2 * nb:, 0:blk], NEG_INF)
        m_a = jnp.max(s_a, axis=0, keepdims=True)
        acc_a = jnp.dot(vt_own[:, 0:blk], jnp.exp2(s_a - m_a).astype(BF16), preferred_element_type=F32)
        s_p = r[hh][2 * nb:, blk:]
        s_o = jnp.where(causal, r2[hh], NEG_INF)
        b_p = bias_ref[hh, 2 * t][:, blk:]
        m_b = jnp.maximum(jnp.max(s_p, axis=0, keepdims=True) + b_p, jnp.max(s_o, axis=0, keepdims=True))
        p_b = jnp.concatenate([jnp.exp2(s_p - (m_b - b_p)).astype(BF16),
                               jnp.exp2(s_o - m_b).astype(BF16)], axis=0)
        acc_b = jnp.dot(vt_own, p_b, preferred_element_type=F32)
        own.append((jnp.concatenate([m_a, m_b], axis=1), jnp.concatenate([acc_a, acc_b], axis=1)))

    def trip(i, carry, slot, prefetch=True):
        state, maxes = carry
        nxt = issue_scores(jnp.minimum(i + 1, nb // 2 - 1), 1 - slot) if prefetch else maxes
        return tuple(softmax_pv(hh, s_slots[slot][hh], maxes[hh], i, state[hh]) for hh in heads), nxt

    def two_trips(u, carry):
        return trip(2 * u + 1, trip(2 * u, carry, 0), 1)

    carry = lax.fori_loop(0, t // 2, two_trips, (tuple(own), maxes0))
    carry = lax.cond(t % 2 == 1, lambda: trip(t - 1, carry, 0, prefetch=False), lambda: carry)
    outs = [acc[0:HEAD_DIM, :] / acc[HEAD_DIM:HEAD_DIM + 1, :] for _, acc in carry[0]]
    o_ref[0] = jnp.concatenate(outs, axis=0).T


def _attn_call(q, k, vt):
    b, s, aw = q.shape
    ngroups = aw // LANES
    nb = s // MOBA_BLOCK
    tq = PAIR_KEYS
    return pl.pallas_call(
        functools.partial(_attn_kernel, nb=nb),
        grid=(b, ngroups, s // tq),
        in_specs=[pl.BlockSpec((1, tq, LANES), lambda bi, gi, t: (bi, t, gi)),
                  pl.BlockSpec((1, s, LANES), lambda bi, gi, t: (bi, 0, gi)),
                  pl.BlockSpec((1, HEADS_PER_GROUP, nb // 2, VT_ROWS, PAIR_KEYS),
                               lambda bi, gi, t: (bi, gi, 0, 0, 0))],
        out_specs=pl.BlockSpec((1, tq, LANES), lambda bi, gi, t: (bi, t, gi)),
        out_shape=jax.ShapeDtypeStruct((b, s, aw), F32),
        scratch_shapes=[pltpu.VMEM((nb, LANES), F32),
                        pltpu.VMEM((HEADS_PER_GROUP, nb, 1, tq), F32),
                        pltpu.VMEM((HEADS_PER_GROUP, PAIR_KEYS, tq), F32),
                        pltpu.VMEM((HEADS_PER_GROUP, PAIR_KEYS, tq), F32)],
        compiler_params=_params("arbitrary", "arbitrary", "arbitrary"),
        name="moba_attn",
    )(q, k, vt)


def _ffn_kernel(ao_ref, pn_ref, xin_ref, ga_ref, wo_ref, gpm_ref, gt1_ref,
                sc_ref, sh_ref, g_ref, wu_ref, cw_ref, cb_ref, wd_ref, gp_ref, gt_ref,
                o_ref, tail_ref, act_ref, *, ts, tf):
    si = pl.program_id(1)
    aw = ao_ref.shape[2]
    an = (_rms(ao_ref[0]) * ga_ref[...]).astype(BF16)
    y = (jnp.dot(an, wo_ref[0:aw, :], preferred_element_type=F32)
         + jnp.dot(pn_ref[0], wo_ref[aw:, :], preferred_element_type=F32))
    x = xin_ref[0] + _rms(y) * (gt1_ref[0] * gpm_ref[...])

    hb = (_rms(x) * (g_ref[...] * (1.0 + sc_ref[0])) + sh_ref[0]).astype(BF16)
    d_ff = wd_ref.shape[0]
    first = si == 0

    def conv(lo):
        u = jnp.dot(hb, wu_ref[:, lo:lo + tf], preferred_element_type=F32)
        tail = jnp.where(first, 0.0, tail_ref[:, lo:lo + tf])
        tail_ref[:, lo:lo + tf] = u[ts - CONV_HALO:, :]
        ucat = jnp.concatenate([tail, u], axis=0)
        uc = cb_ref[:, lo:lo + tf] + cw_ref[CONV_WIDTH - 1:CONV_WIDTH, lo:lo + tf] * u
        for j in range(CONV_WIDTH - 1):
            back = CONV_WIDTH - 1 - j
            uc = uc + cw_ref[j:j + 1, lo:lo + tf] * ucat[CONV_HALO - back:CONV_HALO - back + ts, :]
        return uc

    for fi in range(d_ff // tf):
        a = conv(fi * tf)
        g = conv(d_ff + fi * tf)
        act_ref[:, fi * tf:(fi + 1) * tf] = (a * jax.nn.sigmoid(a) * g).astype(BF16)
    y = jnp.dot(act_ref[...], wd_ref[...], preferred_element_type=F32)
    o_ref[0] = x + _rms(y) * (gt_ref[0] * gp_ref[...])


def _ffn_call(l, ao, pn, x, mod, attn_gain, w_out, g_post_mix,
              g_pre, w_up, conv_w, conv_b, w_down, g_post, ts=512, tf=256):
    b, s, d = x.shape
    f = w_down.shape[1]
    aw = ao.shape[2]
    pw = pn.shape[2]
    tok = lambda bi, si: (bi, si, 0)
    resident = dict(pipeline_mode=pl.Buffered(1))
    return pl.pallas_call(
        functools.partial(_ffn_kernel, ts=ts, tf=tf),
        grid=(b, s // ts),
        in_specs=[pl.BlockSpec((1, ts, aw), tok),
                  pl.BlockSpec((1, ts, pw), tok),
                  pl.BlockSpec((1, ts, d), tok),
                  _layer_spec(l, attn_gain),
                  _layer_spec(l, w_out, **resident),
                  _layer_spec(l, g_post_mix),
                  _mod_spec(l, MOD_GT1, d),
                  _mod_spec(l, MOD_SC2, d),
                  _mod_spec(l, MOD_SH2, d),
                  _layer_spec(l, g_pre),
                  _layer_spec(l, w_up, **resident),
                  _layer_spec(l, conv_w),
                  _layer_spec(l, conv_b),
                  _layer_spec(l, w_down, **resident),
                  _layer_spec(l, g_post),
                  _mod_spec(l, MOD_GT2, d)],
        out_specs=pl.BlockSpec((1, ts, d), tok),
        out_shape=jax.ShapeDtypeStruct((b, s, d), F32),
        scratch_shapes=[pltpu.VMEM((CONV_HALO, 2 * f), F32),
                        pltpu.VMEM((ts, f), BF16)],
        compiler_params=_params("arbitrary", "arbitrary"),
        name="conv_ffn",
    )(ao, pn, x, attn_gain, w_out, g_post_mix, mod,
      mod, mod, g_pre, w_up, conv_w, conv_b, w_down, g_post, mod)


def _rotary_tables(s):
    half = ROT_DIM // 2
    pos = jnp.arange(s, dtype=F32)
    inv_freq = jnp.power(jnp.float32(ROPE_THETA), -jnp.arange(0, ROT_DIM, 2, dtype=F32) / ROT_DIM)
    dim = np.arange(LANES) % HEAD_DIM
    freq = jnp.where(dim < ROT_DIM, inv_freq[dim % half], 0.0)
    ang = pos[:, None] * freq[None, :]
    cos = jnp.cos(ang)
    sin = jnp.sin(ang)
    sa = jnp.where(dim < half, -sin, 0.0)
    sb = jnp.where((dim >= half) & (dim < ROT_DIM), sin, 0.0)
    return cos, sa, sb


def kernel(x, c, w_ada, b_ada, g_pre_mix, w_in, w_pool, pool_scale, attn_out_gain, pool_out_gain,
           w_out, g_post_mix, g_pre_ffn, w_up, conv_w, conv_b, w_down, g_post_ffn):
    b, s, d = x.shape
    depth = w_ada.shape[0]
    assert s % PAIR_KEYS == 0 and d % LANES == 0
    mod = _mod_call(c, w_ada, b_ada).reshape(depth, b, 6, 1, d)
    cos_t, sa_t, sb_t = _rotary_tables(s)
    w_in, w_pool, w_out, w_up, w_down = (w.astype(BF16) for w in (w_in, w_pool, w_out, w_up, w_down))
    (g_pre_mix, pool_scale, pool_out_gain, attn_out_gain, g_post_mix, g_pre_ffn, conv_b,
     g_post_ffn) = (v[:, None, :] for v in (g_pre_mix, pool_scale, pool_out_gain, attn_out_gain,
                                            g_post_mix, g_pre_ffn, conv_b, g_post_ffn))
    for l in range(depth):
        q, k, vt, pn = _in_call(l, x, mod, g_pre_mix, w_in, cos_t, sa_t, sb_t, w_pool,
                                pool_scale, pool_out_gain)
        ao = _attn_call(q, k, vt)
        x = _ffn_call(l, ao, pn, x, mod, attn_out_gain, w_out, g_post_mix,
                      g_pre_ffn, w_up, conv_w, conv_b, w_down, g_post_ffn)
    return x
```

```python
import functools

import jax
import jax.numpy as jnp
import numpy as np
from jax import lax
from jax.experimental import pallas as pl
from jax.experimental.pallas import tpu as pltpu

F32 = jnp.float32
BF16 = jnp.bfloat16

HEAD_DIM = 64
HEADS_PER_GROUP = 2
LANES = 128
MOBA_BLOCK = 256
MOBA_TOP_K = 3
POOL_WINDOWS = (2, 4, 8, 16)
POOL_HALO = 16
ROT_DIM = HEAD_DIM // 4
ROPE_THETA = 500000.0
CONV_WIDTH = 3
CONV_HALO = 8
NORM_EPS = 1e-6
NEG_INF = -1e30
LOG2_E = 1.4426950408889634
VT_ROWS = HEAD_DIM + 16
PAIR_KEYS = 2 * MOBA_BLOCK
VMEM_LIMIT = 48 * 1024 * 1024
MOD_SH1, MOD_SC1, MOD_GT1, MOD_SH2, MOD_SC2, MOD_GT2 = range(6)

_NT = (((1,), (1,)), ((), ()))


def _rms(x):
    return x * lax.rsqrt(jnp.mean(x * x, axis=-1, keepdims=True) + NORM_EPS)


def _params(*sem):
    return pltpu.CompilerParams(dimension_semantics=sem, vmem_limit_bytes=VMEM_LIMIT)


def _mod_kernel(c_ref, w_ref, b_ref, o_ref):
    c = c_ref[...]
    c_act = c * jax.nn.sigmoid(c)
    o_ref[0] = jnp.dot(c_act.astype(BF16), w_ref[0].astype(BF16),
                       preferred_element_type=F32) + b_ref[0]


def _mod_call(c, w_ada, b_ada, tn=1536):
    nl, d, n6 = w_ada.shape
    b = c.shape[0]
    return pl.pallas_call(
        _mod_kernel,
        grid=(nl, n6 // tn),
        in_specs=[pl.BlockSpec((b, d), lambda l, j: (0, 0)),
                  pl.BlockSpec((1, d, tn), lambda l, j: (l, 0, j)),
                  pl.BlockSpec((1, 1, tn), lambda l, j: (l, 0, j))],
        out_specs=pl.BlockSpec((1, b, tn), lambda l, j: (l, 0, j)),
        out_shape=jax.ShapeDtypeStruct((nl, b, n6), F32),
        compiler_params=_params("arbitrary", "arbitrary"),
        name="adaln_mod",
    )(c, w_ada, b_ada.reshape(nl, 1, n6))


def _in_kernel(x_ref, sc_ref, sh_ref, g_ref, w_ref, cos_ref, sa_ref, sb_ref, wp_ref, ps_ref,
               pg_ref, q_ref, k_ref, vt_ref, pn_ref, halo_ref, *, ts, aw):
    si = pl.program_id(1)
    h = _rms(x_ref[0]) * (g_ref[...] * (1.0 + sc_ref[0])) + sh_ref[0]
    z = jnp.dot(h.astype(BF16), w_ref[...], preferred_element_type=F32)

    cos = cos_ref[...]
    sa = sa_ref[...]
    sb = sb_ref[...]

    def rot(t):
        return t * cos + pltpu.roll(t, LANES - ROT_DIM // 2, 1) * sa + pltpu.roll(t, ROT_DIM // 2, 1) * sb

    scale = HEAD_DIM ** -0.5 * LOG2_E
    ngroups = aw // LANES
    ones = jnp.ones((VT_ROWS - HEAD_DIM, PAIR_KEYS), BF16)
    for gi in range(ngroups):
        lo = gi * LANES
        q_ref[0, :, lo:lo + LANES] = (rot(z[:, lo:lo + LANES]) * scale).astype(BF16)
        k_ref[0, :, lo:lo + LANES] = rot(z[:, aw + lo:aw + lo + LANES]).astype(BF16)
        vt = z[:, 2 * aw + lo:2 * aw + lo + LANES].T.astype(BF16)
        for hh in range(HEADS_PER_GROUP):
            head = gi * HEADS_PER_GROUP + hh
            for jp in range(ts // PAIR_KEYS):
                vt_ref[0, head, jp, 0:HEAD_DIM, :] = vt[hh * HEAD_DIM:(hh + 1) * HEAD_DIM,
                                                        jp * PAIR_KEYS:(jp + 1) * PAIR_KEYS]
                vt_ref[0, head, jp, HEAD_DIM:VT_ROWS, :] = ones

    pz = z[:, 3 * aw:]
    prev = jnp.where(si == 0, 0.0, halo_ref[...])
    halo_ref[...] = pz[ts - POOL_HALO:, :]
    ext = jnp.concatenate([prev, pz], axis=0)
    tpos = si * ts + lax.broadcasted_iota(jnp.int32, (ts, 1), 0)
    outs = []
    for g, w in enumerate(POOL_WINDOWS):
        lo = g * LANES
        s = ext[:, lo:lo + LANES]
        span = 1
        while span < w:
            s = s[span:] + s[:-span]
            span *= 2
        s = s[POOL_HALO - (w - 1):POOL_HALO - (w - 1) + ts]
        cnt = jnp.minimum(tpos + 1, w).astype(F32)
        pooled = s / cnt - pz[:, lo:lo + LANES]
        outs.append(jnp.dot(pooled.astype(BF16), wp_ref[g], preferred_element_type=F32))
    po = jnp.concatenate(outs, axis=1) * ps_ref[...]
    pn_ref[0] = (_rms(po) * pg_ref[...]).astype(BF16)


def _mod_spec(l, chunk, d):
    return pl.BlockSpec((None, None, 1, 1, d), lambda bi, si: (l, bi, chunk, 0, 0))


def _layer_spec(l, arr, **kw):
    return pl.BlockSpec((None,) + arr.shape[1:], lambda bi, si: (l,) + (0,) * (arr.ndim - 1), **kw)


def _in_call(l, x, mod, g, w_in, cos_t, sa_t, sb_t, w_pool, pool_scale, pool_gain, ts=512):
    b, s, d = x.shape
    n_in = w_in.shape[2]
    pw = w_pool.shape[1] * w_pool.shape[2]
    aw = (n_in - pw) // 3
    nheads = aw // HEAD_DIM
    tok = lambda bi, si: (bi, si, 0)
    return pl.pallas_call(
        functools.partial(_in_kernel, ts=ts, aw=aw),
        grid=(b, s // ts),
        in_specs=[pl.BlockSpec((1, ts, d), tok),
                  _mod_spec(l, MOD_SC1, d),
                  _mod_spec(l, MOD_SH1, d),
                  _layer_spec(l, g),
                  _layer_spec(l, w_in),
                  pl.BlockSpec((ts, LANES), lambda bi, si: (si, 0)),
                  pl.BlockSpec((ts, LANES), lambda bi, si: (si, 0)),
                  pl.BlockSpec((ts, LANES), lambda bi, si: (si, 0)),
                  _layer_spec(l, w_pool),
                  _layer_spec(l, pool_scale),
                  _layer_spec(l, pool_gain)],
        out_specs=[pl.BlockSpec((1, ts, aw), tok),
                   pl.BlockSpec((1, ts, aw), tok),
                   pl.BlockSpec((1, nheads, ts // PAIR_KEYS, VT_ROWS, PAIR_KEYS),
                                lambda bi, si: (bi, 0, si, 0, 0)),
                   pl.BlockSpec((1, ts, pw), tok)],
        out_shape=[jax.ShapeDtypeStruct((b, s, aw), BF16),
                   jax.ShapeDtypeStruct((b, s, aw), BF16),
                   jax.ShapeDtypeStruct((b, nheads, s // PAIR_KEYS, VT_ROWS, PAIR_KEYS), BF16),
                   jax.ShapeDtypeStruct((b, s, pw), BF16)],
        scratch_shapes=[pltpu.VMEM((POOL_HALO, pw), F32)],
        compiler_params=_params("arbitrary", "arbitrary"),
        name="in_proj",
    )(x, mod, mod, g, w_in, cos_t, sa_t, sb_t, w_pool, pool_scale, pool_gain)


def _attn_kernel(q_ref, k_ref, vt_ref, o_ref, km_ref, bias_ref, s0_ref, s1_ref, *, nb):
    s_slots = (s0_ref, s1_ref)
    t = pl.program_id(2)
    blk = MOBA_BLOCK
    tq = PAIR_KEYS
    heads = range(HEADS_PER_GROUP)

    @pl.when(t == 0)
    def _():
        for j in range(nb):
            kj = k_ref[0, j * blk:(j + 1) * blk, :].astype(F32)
            km_ref[j:j + 1, :] = jnp.mean(kj, axis=0, keepdims=True)

    q2 = q_ref[0]
    lane = lax.broadcasted_iota(jnp.int32, (1, LANES), 1)
    qh = [jnp.where((lane >= hh * HEAD_DIM) & (lane < (hh + 1) * HEAD_DIM), q2, jnp.zeros_like(q2))
          for hh in heads]

    def pair_keys(pi):
        return k_ref[0, pl.ds(pl.multiple_of(pi * PAIR_KEYS, PAIR_KEYS), PAIR_KEYS), :]

    def issue_scores(j, slot):
        kj = k_ref[0, pl.ds(pl.multiple_of(j * blk, blk), blk), :]
        maxes = []
        for hh in heads:
            s = lax.dot_general(kj, qh[hh], _NT, preferred_element_type=F32)
            s_slots[slot][hh] = s
            maxes.append(jnp.max(s, axis=0, keepdims=True))
        return tuple(maxes)

    def softmax_pv(hh, s, cmax, j, vt_blk, state):
        b = bias_ref[hh, j]
        m_new = jnp.maximum(state[0], cmax + b)
        p = jnp.exp2(s - (m_new - b)).astype(BF16)
        acc = jnp.exp2(state[0] - m_new) * state[1] + jnp.dot(vt_blk, p, preferred_element_type=F32)
        return m_new, acc

    km = km_ref[...]
    km_hi = km.astype(BF16)
    km_lo = (km - km_hi.astype(F32)).astype(BF16)
    kown = pair_keys(t)
    lhs = jnp.concatenate([km_hi, km_lo, kown[0:blk]], axis=0)
    r = [lax.dot_general(lhs, qh[hh], _NT, preferred_element_type=F32) for hh in heads]
    r2 = [lax.dot_general(kown[blk:], qh[hh][blk:], _NT, preferred_element_type=F32) for hh in heads]
    maxes0 = issue_scores(0, 0)

    jidx = lax.broadcasted_iota(jnp.int32, (nb, 1), 0)
    qblk = 2 * t + lax.broadcasted_iota(jnp.int32, (1, tq), 1) // blk
    past = jidx < qblk
    for hh in heads:
        gate = jnp.where(past, r[hh][0:nb] + r[hh][nb:2 * nb], NEG_INF)
        picked = jnp.zeros(gate.shape, jnp.bool_)
        for _ in range(min(MOBA_TOP_K, nb)):
            top = jnp.max(gate, axis=0, keepdims=True)
            first = jnp.min(jnp.where(gate == top, jidx, nb), axis=0, keepdims=True)
            hit = jidx == first
            picked = picked | hit
            gate = jnp.where(hit, -jnp.inf, gate)
        bias = jnp.where(picked & past, 0.0, NEG_INF).astype(F32)
        for j in range(nb):
            bias_ref[hh, j] = bias[j:j + 1, :]

    causal = (lax.broadcasted_iota(jnp.int32, (blk, blk), 0)
              <= lax.broadcasted_iota(jnp.int32, (blk, blk), 1))
    own = []
    for hh in heads:
        vt_own = vt_ref[0, hh, t]
        s_a = jnp.where(causal, r[hh][2 * nb:, 0:blk], NEG_INF)
        m_a = jnp.max(s_a, axis=0, keepdims=True)
        acc_a = jnp.dot(vt_own[:, 0:blk], jnp.exp2(s_a - m_a).astype(BF16), preferred_element_type=F32)
        s_p = r[hh][2 * nb:, blk:]
        s_o = jnp.where(causal, r2[hh], NEG_INF)
        b_p = bias_ref[hh, 2 * t][:, blk:]
        m_b = jnp.maximum(jnp.max(s_p, axis=0, keepdims=True) + b_p, jnp.max(s_o, axis=0, keepdims=True))
        p_b = jnp.concatenate([jnp.exp2(s_p - (m_b - b_p)).astype(BF16),
                               jnp.exp2(s_o - m_b).astype(BF16)], axis=0)
        acc_b = jnp.dot(vt_own, p_b, preferred_element_type=F32)
        own.append((jnp.concatenate([m_a, m_b], axis=1), jnp.concatenate([acc_a, acc_b], axis=1)))

    def trip(j, carry, slot, vt_blocks):
        state, maxes = carry
        nxt = issue_scores(jnp.minimum(j + 1, nb - 1), 1 - slot)
        return tuple(softmax_pv(hh, s_slots[slot][hh], maxes[hh], j, vt_blocks[hh], state[hh])
                     for hh in heads), nxt

    def pair_trips(u, carry):
        vt_pair = [vt_ref[0, hh, u] for hh in heads]
        carry = trip(2 * u, carry, 0, [v[:, 0:blk] for v in vt_pair])
        return trip(2 * u + 1, carry, 1, [v[:, blk:] for v in vt_pair])

    def two_pairs(v, carry):
        return pair_trips(2 * v + 1, pair_trips(2 * v, carry))

    carry = lax.fori_loop(0, t // 2, two_pairs, (tuple(own), maxes0))
    carry = lax.cond(t % 2 == 1, lambda: pair_trips(t - 1, carry), lambda: carry)
    outs = [acc[0:HEAD_DIM, :] / acc[HEAD_DIM:HEAD_DIM + 1, :] for _, acc in carry[0]]
    o_ref[0] = jnp.concatenate(outs, axis=0).T


def _attn_call(q, k, vt):
    b, s, aw = q.shape
    ngroups = aw // LANES
    nb = s // MOBA_BLOCK
    tq = PAIR_KEYS
    return pl.pallas_call(
        functools.partial(_attn_kernel, nb=nb),
        grid=(b, ngroups, s // tq),
        in_specs=[pl.BlockSpec((1, tq, LANES), lambda bi, gi, t: (bi, t, gi)),
                  pl.BlockSpec((1, s, LANES), lambda bi, gi, t: (bi, 0, gi)),
                  pl.BlockSpec((1, HEADS_PER_GROUP, nb // 2, VT_ROWS, PAIR_KEYS),
                               lambda bi, gi, t: (bi, gi, 0, 0, 0))],
        out_specs=pl.BlockSpec((1, tq, LANES), lambda bi, gi, t: (bi, t, gi)),
        out_shape=jax.ShapeDtypeStruct((b, s, aw), F32),
        scratch_shapes=[pltpu.VMEM((nb, LANES), F32),
                        pltpu.VMEM((HEADS_PER_GROUP, nb, 1, tq), F32),
                        pltpu.VMEM((HEADS_PER_GROUP, MOBA_BLOCK, tq), F32),
                        pltpu.VMEM((HEADS_PER_GROUP, MOBA_BLOCK, tq), F32)],
        compiler_params=_params("arbitrary", "arbitrary", "arbitrary"),
        name="moba_attn",
    )(q, k, vt)


def _ffn_kernel(ao_ref, pn_ref, xin_ref, ga_ref, wo_ref, gpm_ref, gt1_ref,
                sc_ref, sh_ref, g_ref, wu_ref, cw_ref, cb_ref, wd_ref, gp_ref, gt_ref,
                o_ref, tail_ref, act_ref, *, ts, tf):
    si = pl.program_id(1)
    aw = ao_ref.shape[2]
    an = (_rms(ao_ref[0]) * ga_ref[...]).astype(BF16)
    y = (jnp.dot(an, wo_ref[0:aw, :], preferred_element_type=F32)
         + jnp.dot(pn_ref[0], wo_ref[aw:, :], preferred_element_type=F32))
    x = xin_ref[0] + _rms(y) * (gt1_ref[0] * gpm_ref[...])

    hb = (_rms(x) * (g_ref[...] * (1.0 + sc_ref[0])) + sh_ref[0]).astype(BF16)
    d_ff = wd_ref.shape[0]
    first = si == 0

    def conv(lo):
        u = jnp.dot(hb, wu_ref[:, lo:lo + tf], preferred_element_type=F32)
        tail = jnp.where(first, 0.0, tail_ref[:, lo:lo + tf])
        tail_ref[:, lo:lo + tf] = u[ts - CONV_HALO:, :]
        ucat = jnp.concatenate([tail, u], axis=0)
        uc = cb_ref[:, lo:lo + tf] + cw_ref[CONV_WIDTH - 1:CONV_WIDTH, lo:lo + tf] * u
        for j in range(CONV_WIDTH - 1):
            back = CONV_WIDTH - 1 - j
            uc = uc + cw_ref[j:j + 1, lo:lo + tf] * ucat[CONV_HALO - back:CONV_HALO - back + ts, :]
        return uc

    for fi in range(d_ff // tf):
        a = conv(fi * tf)
        g = conv(d_ff + fi * tf)
        act_ref[:, fi * tf:(fi + 1) * tf] = (a * jax.nn.sigmoid(a) * g).astype(BF16)
    y = jnp.dot(act_ref[...], wd_ref[...], preferred_element_type=F32)
    o_ref[0] = x + _rms(y) * (gt_ref[0] * gp_ref[...])


def _ffn_call(l, ao, pn, x, mod, attn_gain, w_out, g_post_mix,
              g_pre, w_up, conv_w, conv_b, w_down, g_post, ts=512, tf=256):
    b, s, d = x.shape
    f = w_down.shape[1]
    aw = ao.shape[2]
    pw = pn.shape[2]
    tok = lambda bi, si: (bi, si, 0)
    resident = dict(pipeline_mode=pl.Buffered(1))
    return pl.pallas_call(
        functools.partial(_ffn_kernel, ts=ts, tf=tf),
        grid=(b, s // ts),
        in_specs=[pl.BlockSpec((1, ts, aw), tok),
                  pl.BlockSpec((1, ts, pw), tok),
                  pl.BlockSpec((1, ts, d), tok),
                  _layer_spec(l, attn_gain),
                  _layer_spec(l, w_out, **resident),
                  _layer_spec(l, g_post_mix),
                  _mod_spec(l, MOD_GT1, d),
                  _mod_spec(l, MOD_SC2, d),
                  _mod_spec(l, MOD_SH2, d),
                  _layer_spec(l, g_pre),
                  _layer_spec(l, w_up, **resident),
                  _layer_spec(l, conv_w),
                  _layer_spec(l, conv_b),
                  _layer_spec(l, w_down, **resident),
                  _layer_spec(l, g_post),
                  _mod_spec(l, MOD_GT2, d)],
        out_specs=pl.BlockSpec((1, ts, d), tok),
        out_shape=jax.ShapeDtypeStruct((b, s, d), F32),
        scratch_shapes=[pltpu.VMEM((CONV_HALO, 2 * f), F32),
                        pltpu.VMEM((ts, f), BF16)],
        compiler_params=_params("arbitrary", "arbitrary"),
        name="conv_ffn",
    )(ao, pn, x, attn_gain, w_out, g_post_mix, mod,
      mod, mod, g_pre, w_up, conv_w, conv_b, w_down, g_post, mod)


def _rotary_tables(s):
    half = ROT_DIM // 2
    pos = jnp.arange(s, dtype=F32)
    inv_freq = jnp.power(jnp.float32(ROPE_THETA), -jnp.arange(0, ROT_DIM, 2, dtype=F32) / ROT_DIM)
    dim = np.arange(LANES) % HEAD_DIM
    freq = jnp.where(dim < ROT_DIM, inv_freq[dim % half], 0.0)
    ang = pos[:, None] * freq[None, :]
    cos = jnp.cos(ang)
    sin = jnp.sin(ang)
    sa = jnp.where(dim < half, -sin, 0.0)
    sb = jnp.where((dim >= half) & (dim < ROT_DIM), sin, 0.0)
    return cos, sa, sb


def kernel(x, c, w_ada, b_ada, g_pre_mix, w_in, w_pool, pool_scale, attn_out_gain, pool_out_gain,
           w_out, g_post_mix, g_pre_ffn, w_up, conv_w, conv_b, w_down, g_post_ffn):
    b, s, d = x.shape
    depth = w_ada.shape[0]
    assert s % PAIR_KEYS == 0 and d % LANES == 0
    mod = _mod_call(c, w_ada, b_ada).reshape(depth, b, 6, 1, d)
    cos_t, sa_t, sb_t = _rotary_tables(s)
    w_in, w_pool, w_out, w_up, w_down = (w.astype(BF16) for w in (w_in, w_pool, w_out, w_up, w_down))
    (g_pre_mix, pool_scale, pool_out_gain, attn_out_gain, g_post_mix, g_pre_ffn, conv_b,
     g_post_ffn) = (v[:, None, :] for v in (g_pre_mix, pool_scale, pool_out_gain, attn_out_gain,
                                            g_post_mix, g_pre_ffn, conv_b, g_post_ffn))
    for l in range(depth):
        q, k, vt, pn = _in_call(l, x, mod, g_pre_mix, w_in, cos_t, sa_t, sb_t, w_pool,
                                pool_scale, pool_out_gain)
        ao = _attn_call(q, k, vt)
        x = _ffn_call(l, ao, pn, x, mod, attn_out_gain, w_out, g_post_mix,
                      g_pre_ffn, w_up, conv_w, conv_b, w_down, g_post_ffn)
    return x
```

```python
import functools

import jax
import jax.numpy as jnp
import numpy as np
from jax import lax
from jax.experimental import pallas as pl
from jax.experimental.pallas import tpu as pltpu

F32 = jnp.float32
BF16 = jnp.bfloat16

HEAD_DIM = 64
HEADS_PER_GROUP = 2
LANES = 128
MOBA_BLOCK = 256
MOBA_TOP_K = 3
POOL_WINDOWS = (2, 4, 8, 16)
POOL_HALO = 16
ROT_DIM = HEAD_DIM // 4
ROPE_THETA = 500000.0
CONV_WIDTH = 3
CONV_HALO = 8
NORM_EPS = 1e-6
NEG_INF = -1e30
LOG2_E = 1.4426950408889634
VT_ROWS = HEAD_DIM + 16
PAIR_KEYS = 2 * MOBA_BLOCK
VMEM_LIMIT = 48 * 1024 * 1024
MOD_SH1, MOD_SC1, MOD_GT1, MOD_SH2, MOD_SC2, MOD_GT2 = range(6)

_NT = (((1,), (1,)), ((), ()))


def _rms(x):
    return x * lax.rsqrt(jnp.mean(x * x, axis=-1, keepdims=True) + NORM_EPS)


def _params(*sem):
    return pltpu.CompilerParams(dimension_semantics=sem, vmem_limit_bytes=VMEM_LIMIT)


def _mod_kernel(c_ref, w_ref, b_ref, o_ref):
    c = c_ref[...]
    c_act = c * jax.nn.sigmoid(c)
    o_ref[0] = jnp.dot(c_act.astype(BF16), w_ref[0].astype(BF16),
                       preferred_element_type=F32) + b_ref[0]


def _mod_call(c, w_ada, b_ada, tn=1536):
    nl, d, n6 = w_ada.shape
    b = c.shape[0]
    return pl.pallas_call(
        _mod_kernel,
        grid=(nl, n6 // tn),
        in_specs=[pl.BlockSpec((b, d), lambda l, j: (0, 0)),
                  pl.BlockSpec((1, d, tn), lambda l, j: (l, 0, j)),
                  pl.BlockSpec((1, 1, tn), lambda l, j: (l, 0, j))],
        out_specs=pl.BlockSpec((1, b, tn), lambda l, j: (l, 0, j)),
        out_shape=jax.ShapeDtypeStruct((nl, b, n6), F32),
        compiler_params=_params("arbitrary", "arbitrary"),
        name="adaln_mod",
    )(c, w_ada, b_ada.reshape(nl, 1, n6))


def _in_kernel(x_ref, sc_ref, sh_ref, g_ref, w_ref, cos_ref, sa_ref, sb_ref, wp_ref, ps_ref,
               pg_ref, q_ref, k_ref, vt_ref, pn_ref, halo_ref, *, ts, aw):
    si = pl.program_id(1)
    h = _rms(x_ref[0]) * (g_ref[...] * (1.0 + sc_ref[0])) + sh_ref[0]
    z = jnp.dot(h.astype(BF16), w_ref[...], preferred_element_type=F32)

    cos = cos_ref[...]
    sa = sa_ref[...]
    sb = sb_ref[...]

    def rot(t):
        return t * cos + pltpu.roll(t, LANES - ROT_DIM // 2, 1) * sa + pltpu.roll(t, ROT_DIM // 2, 1) * sb

    scale = HEAD_DIM ** -0.5 * LOG2_E
    ngroups = aw // LANES
    ones = jnp.ones((VT_ROWS - HEAD_DIM, PAIR_KEYS), BF16)
    for gi in range(ngroups):
        lo = gi * LANES
        q_ref[0, :, lo:lo + LANES] = (rot(z[:, lo:lo + LANES]) * scale).astype(BF16)
        k_ref[0, :, lo:lo + LANES] = rot(z[:, aw + lo:aw + lo + LANES]).astype(BF16)
        vt = z[:, 2 * aw + lo:2 * aw + lo + LANES].T.astype(BF16)
        for hh in range(HEADS_PER_GROUP):
            head = gi * HEADS_PER_GROUP + hh
            for jp in range(ts // PAIR_KEYS):
                vt_ref[0, head, jp, 0:HEAD_DIM, :] = vt[hh * HEAD_DIM:(hh + 1) * HEAD_DIM,
                                                        jp * PAIR_KEYS:(jp + 1) * PAIR_KEYS]
                vt_ref[0, head, jp, HEAD_DIM:VT_ROWS, :] = ones

    pz = z[:, 3 * aw:]
    prev = jnp.where(si == 0, 0.0, halo_ref[...])
    halo_ref[...] = pz[ts - POOL_HALO:, :]
    ext = jnp.concatenate([prev, pz], axis=0)
    tpos = si * ts + lax.broadcasted_iota(jnp.int32, (ts, 1), 0)
    outs = []
    for g, w in enumerate(POOL_WINDOWS):
        lo = g * LANES
        s = ext[:, lo:lo + LANES]
        span = 1
        while span < w:
            s = s[span:] + s[:-span]
            span *= 2
        s = s[POOL_HALO - (w - 1):POOL_HALO - (w - 1) + ts]
        cnt = jnp.minimum(tpos + 1, w).astype(F32)
        pooled = s / cnt - pz[:, lo:lo + LANES]
        outs.append(jnp.dot(pooled.astype(BF16), wp_ref[g], preferred_element_type=F32))
    po = jnp.concatenate(outs, axis=1) * ps_ref[...]
    pn_ref[0] = (_rms(po) * pg_ref[...]).astype(BF16)


def _mod_spec(l, chunk, d):
    return pl.BlockSpec((None, None, 1, 1, d), lambda bi, si: (l, bi, chunk, 0, 0))


def _layer_spec(l, arr, **kw):
    return pl.BlockSpec((None,) + arr.shape[1:], lambda bi, si: (l,) + (0,) * (arr.ndim - 1), **kw)


def _in_call(l, x, mod, g, w_in, cos_t, sa_t, sb_t, w_pool, pool_scale, pool_gain, ts=512):
    b, s, d = x.shape
    n_in = w_in.shape[2]
    pw = w_pool.shape[1] * w_pool.shape[2]
    aw = (n_in - pw) // 3
    nheads = aw // HEAD_DIM
    tok = lambda bi, si: (bi, si, 0)
    return pl.pallas_call(
        functools.partial(_in_kernel, ts=ts, aw=aw),
        grid=(b, s // ts),
        in_specs=[pl.BlockSpec((1, ts, d), tok),
                  _mod_spec(l, MOD_SC1, d),
                  _mod_spec(l, MOD_SH1, d),
                  _layer_spec(l, g),
                  _layer_spec(l, w_in),
                  pl.BlockSpec((ts, LANES), lambda bi, si: (si, 0)),
                  pl.BlockSpec((ts, LANES), lambda bi, si: (si, 0)),
                  pl.BlockSpec((ts, LANES), lambda bi, si: (si, 0)),
                  _layer_spec(l, w_pool),
                  _layer_spec(l, pool_scale),
                  _layer_spec(l, pool_gain)],
        out_specs=[pl.BlockSpec((1, ts, aw), tok),
                   pl.BlockSpec((1, ts, aw), tok),
                   pl.BlockSpec((1, nheads, ts // PAIR_KEYS, VT_ROWS, PAIR_KEYS),
                                lambda bi, si: (bi, 0, si, 0, 0)),
                   pl.BlockSpec((1, ts, pw), tok)],
        out_shape=[jax.ShapeDtypeStruct((b, s, aw), BF16),
                   jax.ShapeDtypeStruct((b, s, aw), BF16),
                   jax.ShapeDtypeStruct((b, nheads, s // PAIR_KEYS, VT_ROWS, PAIR_KEYS), BF16),
                   jax.ShapeDtypeStruct((b, s, pw), BF16)],
        scratch_shapes=[pltpu.VMEM((POOL_HALO, pw), F32)],
        compiler_params=_params("arbitrary", "arbitrary"),
        name="in_proj",
    )(x, mod, mod, g, w_in, cos_t, sa_t, sb_t, w_pool, pool_scale, pool_gain)


def _attn_kernel(q_ref, k_ref, vt_ref, o_ref, km_ref, bias_ref, s0_ref, s1_ref, *, nb):
    blk = MOBA_BLOCK
    for j in range(nb):
        kj = k_ref[0, j * blk:(j + 1) * blk, :].astype(F32)
        km_ref[j:j + 1, :] = jnp.mean(kj, axis=0, keepdims=True)
    s_slots = (s0_ref, s1_ref)

    def tile(t, carry):
        _attn_tile(t, q_ref, k_ref, vt_ref, o_ref, km_ref, bias_ref, s_slots, nb)
        return carry

    lax.fori_loop(0, nb // 2, tile, 0)


def _attn_tile(t, q_ref, k_ref, vt_ref, o_ref, km_ref, bias_ref, s_slots, nb):
    blk = MOBA_BLOCK
    tq = PAIR_KEYS
    heads = range(HEADS_PER_GROUP)
    rows = pl.ds(pl.multiple_of(t * tq, tq), tq)
    q2 = q_ref[0, rows, :]
    lane = lax.broadcasted_iota(jnp.int32, (1, LANES), 1)
    qh = [jnp.where((lane >= hh * HEAD_DIM) & (lane < (hh + 1) * HEAD_DIM), q2, jnp.zeros_like(q2))
          for hh in heads]

    def pair_keys(pi):
        return k_ref[0, pl.ds(pl.multiple_of(pi * PAIR_KEYS, PAIR_KEYS), PAIR_KEYS), :]

    def issue_scores(j, slot):
        kj = k_ref[0, pl.ds(pl.multiple_of(j * blk, blk), blk), :]
        maxes = []
        for hh in heads:
            s = lax.dot_general(kj, qh[hh], _NT, preferred_element_type=F32)
            s_slots[slot][hh] = s
            maxes.append(jnp.max(s, axis=0, keepdims=True))
        return tuple(maxes)

    def softmax_pv(hh, s, cmax, j, vt_blk, state):
        b = bias_ref[hh, j]
        m_new = jnp.maximum(state[0], cmax + b)
        p = jnp.exp2(s - (m_new - b)).astype(BF16)
        acc = jnp.exp2(state[0] - m_new) * state[1] + jnp.dot(vt_blk, p, preferred_element_type=F32)
        return m_new, acc

    km = km_ref[...]
    km_hi = km.astype(BF16)
    km_lo = (km - km_hi.astype(F32)).astype(BF16)
    kown = pair_keys(t)
    lhs = jnp.concatenate([km_hi, km_lo, kown[0:blk]], axis=0)
    r = [lax.dot_general(lhs, qh[hh], _NT, preferred_element_type=F32) for hh in heads]
    r2 = [lax.dot_general(kown[blk:], qh[hh][blk:], _NT, preferred_element_type=F32) for hh in heads]
    maxes0 = issue_scores(0, 0)

    jidx = lax.broadcasted_iota(jnp.int32, (nb, 1), 0)
    qblk = 2 * t + lax.broadcasted_iota(jnp.int32, (1, tq), 1) // blk
    past = jidx < qblk
    for hh in heads:
        gate = jnp.where(past, r[hh][0:nb] + r[hh][nb:2 * nb], NEG_INF)
        picked = jnp.zeros(gate.shape, jnp.bool_)
        for _ in range(min(MOBA_TOP_K, nb)):
            top = jnp.max(gate, axis=0, keepdims=True)
            first = jnp.min(jnp.where(gate == top, jidx, nb), axis=0, keepdims=True)
            hit = jidx == first
            picked = picked | hit
            gate = jnp.where(hit, -jnp.inf, gate)
        bias = jnp.where(picked & past, 0.0, NEG_INF).astype(F32)
        for j in range(nb):
            bias_ref[hh, j] = bias[j:j + 1, :]

    causal = (lax.broadcasted_iota(jnp.int32, (blk, blk), 0)
              <= lax.broadcasted_iota(jnp.int32, (blk, blk), 1))
    own = []
    for hh in heads:
        vt_own = vt_ref[0, hh, t]
        s_a = jnp.where(causal, r[hh][2 * nb:, 0:blk], NEG_INF)
        m_a = jnp.max(s_a, axis=0, keepdims=True)
        acc_a = jnp.dot(vt_own[:, 0:blk], jnp.exp2(s_a - m_a).astype(BF16), preferred_element_type=F32)
        s_p = r[hh][2 * nb:, blk:]
        s_o = jnp.where(causal, r2[hh], NEG_INF)
        b_p = bias_ref[hh, 2 * t][:, blk:]
        m_b = jnp.maximum(jnp.max(s_p, axis=0, keepdims=True) + b_p, jnp.max(s_o, axis=0, keepdims=True))
        p_b = jnp.concatenate([jnp.exp2(s_p - (m_b - b_p)).astype(BF16),
                               jnp.exp2(s_o - m_b).astype(BF16)], axis=0)
        acc_b = jnp.dot(vt_own, p_b, preferred_element_type=F32)
        own.append((jnp.concatenate([m_a, m_b], axis=1), jnp.concatenate([acc_a, acc_b], axis=1)))

    def trip(j, carry, slot, vt_blocks):
        state, maxes = carry
        nxt = issue_scores(jnp.minimum(j + 1, nb - 1), 1 - slot)
        return tuple(softmax_pv(hh, s_slots[slot][hh], maxes[hh], j, vt_blocks[hh], state[hh])
                     for hh in heads), nxt

    def pair_trips(u, carry):
        vt_pair = [vt_ref[0, hh, u] for hh in heads]
        carry = trip(2 * u, carry, 0, [v[:, 0:blk] for v in vt_pair])
        return trip(2 * u + 1, carry, 1, [v[:, blk:] for v in vt_pair])

    def two_pairs(v, carry):
        return pair_trips(2 * v + 1, pair_trips(2 * v, carry))

    carry = lax.fori_loop(0, t // 2, two_pairs, (tuple(own), maxes0))
    carry = lax.cond(t % 2 == 1, lambda: pair_trips(t - 1, carry), lambda: carry)
    outs = [acc[0:HEAD_DIM, :] / acc[HEAD_DIM:HEAD_DIM + 1, :] for _, acc in carry[0]]
    o_ref[0, rows, :] = jnp.concatenate(outs, axis=0).T


def _attn_call(q, k, vt):
    b, s, aw = q.shape
    ngroups = aw // LANES
    nb = s // MOBA_BLOCK
    tq = PAIR_KEYS
    return pl.pallas_call(
        functools.partial(_attn_kernel, nb=nb),
        grid=(b, ngroups),
        in_specs=[pl.BlockSpec((1, s, LANES), lambda bi, gi: (bi, 0, gi)),
                  pl.BlockSpec((1, s, LANES), lambda bi, gi: (bi, 0, gi)),
                  pl.BlockSpec((1, HEADS_PER_GROUP, nb // 2, VT_ROWS, PAIR_KEYS),
                               lambda bi, gi: (bi, gi, 0, 0, 0))],
        out_specs=pl.BlockSpec((1, s, LANES), lambda bi, gi: (bi, 0, gi)),
        out_shape=jax.ShapeDtypeStruct((b, s, aw), F32),
        scratch_shapes=[pltpu.VMEM((nb, LANES), F32),
                        pltpu.VMEM((HEADS_PER_GROUP, nb, 1, tq), F32),
                        pltpu.VMEM((HEADS_PER_GROUP, MOBA_BLOCK, tq), F32),
                        pltpu.VMEM((HEADS_PER_GROUP, MOBA_BLOCK, tq), F32)],
        compiler_params=_params("arbitrary", "arbitrary"),
        name="moba_attn",
    )(q, k, vt)


def _ffn_kernel(ao_ref, pn_ref, xin_ref, ga_ref, wo_ref, gpm_ref, gt1_ref,
                sc_ref, sh_ref, g_ref, wu_ref, cw_ref, cb_ref, wd_ref, gp_ref, gt_ref,
                o_ref, tail_ref, act_ref, *, ts, tf):
    si = pl.program_id(1)
    aw = ao_ref.shape[2]
    an = (_rms(ao_ref[0]) * ga_ref[...]).astype(BF16)
    y = (jnp.dot(an, wo_ref[0:aw, :], preferred_element_type=F32)
         + jnp.dot(pn_ref[0], wo_ref[aw:, :], preferred_element_type=F32))
    x = xin_ref[0] + _rms(y) * (gt1_ref[0] * gpm_ref[...])

    hb = (_rms(x) * (g_ref[...] * (1.0 + sc_ref[0])) + sh_ref[0]).astype(BF16)
    d_ff = wd_ref.shape[0]
    first = si == 0

    def conv(lo):
        u = jnp.dot(hb, wu_ref[:, lo:lo + tf], preferred_element_type=F32)
        tail = jnp.where(first, 0.0, tail_ref[:, lo:lo + tf])
        tail_ref[:, lo:lo + tf] = u[ts - CONV_HALO:, :]
        ucat = jnp.concatenate([tail, u], axis=0)
        uc = cb_ref[:, lo:lo + tf] + cw_ref[CONV_WIDTH - 1:CONV_WIDTH, lo:lo + tf] * u
        for j in range(CONV_WIDTH - 1):
            back = CONV_WIDTH - 1 - j
            uc = uc + cw_ref[j:j + 1, lo:lo + tf] * ucat[CONV_HALO - back:CONV_HALO - back + ts, :]
        return uc

    for fi in range(d_ff // tf):
        a = conv(fi * tf)
        g = conv(d_ff + fi * tf)
        act_ref[:, fi * tf:(fi + 1) * tf] = (a * jax.nn.sigmoid(a) * g).astype(BF16)
    y = jnp.dot(act_ref[...], wd_ref[...], preferred_element_type=F32)
    o_ref[0] = x + _rms(y) * (gt_ref[0] * gp_ref[...])


def _ffn_call(l, ao, pn, x, mod, attn_gain, w_out, g_post_mix,
              g_pre, w_up, conv_w, conv_b, w_down, g_post, ts=512, tf=256):
    b, s, d = x.shape
    f = w_down.shape[1]
    aw = ao.shape[2]
    pw = pn.shape[2]
    tok = lambda bi, si: (bi, si, 0)
    resident = dict(pipeline_mode=pl.Buffered(1))
    return pl.pallas_call(
        functools.partial(_ffn_kernel, ts=ts, tf=tf),
        grid=(b, s // ts),
        in_specs=[pl.BlockSpec((1, ts, aw), tok),
                  pl.BlockSpec((1, ts, pw), tok),
                  pl.BlockSpec((1, ts, d), tok),
                  _layer_spec(l, attn_gain),
                  _layer_spec(l, w_out, **resident),
                  _layer_spec(l, g_post_mix),
                  _mod_spec(l, MOD_GT1, d),
                  _mod_spec(l, MOD_SC2, d),
                  _mod_spec(l, MOD_SH2, d),
                  _layer_spec(l, g_pre),
                  _layer_spec(l, w_up, **resident),
                  _layer_spec(l, conv_w),
                  _layer_spec(l, conv_b),
                  _layer_spec(l, w_down, **resident),
                  _layer_spec(l, g_post),
                  _mod_spec(l, MOD_GT2, d)],
        out_specs=pl.BlockSpec((1, ts, d), tok),
        out_shape=jax.ShapeDtypeStruct((b, s, d), F32),
        scratch_shapes=[pltpu.VMEM((CONV_HALO, 2 * f), F32),
                        pltpu.VMEM((ts, f), BF16)],
        compiler_params=_params("arbitrary", "arbitrary"),
        name="conv_ffn",
    )(ao, pn, x, attn_gain, w_out, g_post_mix, mod,
      mod, mod, g_pre, w_up, conv_w, conv_b, w_down, g_post, mod)


def _rotary_tables(s):
    half = ROT_DIM // 2
    pos = jnp.arange(s, dtype=F32)
    inv_freq = jnp.power(jnp.float32(ROPE_THETA), -jnp.arange(0, ROT_DIM, 2, dtype=F32) / ROT_DIM)
    dim = np.arange(LANES) % HEAD_DIM
    freq = jnp.where(dim < ROT_DIM, inv_freq[dim % half], 0.0)
    ang = pos[:, None] * freq[None, :]
    cos = jnp.cos(ang)
    sin = jnp.sin(ang)
    sa = jnp.where(dim < half, -sin, 0.0)
    sb = jnp.where((dim >= half) & (dim < ROT_DIM), sin, 0.0)
    return cos, sa, sb


def kernel(x, c, w_ada, b_ada, g_pre_mix, w_in, w_pool, pool_scale, attn_out_gain, pool_out_gain,
           w_out, g_post_mix, g_pre_ffn, w_up, conv_w, conv_b, w_down, g_post_ffn):
    b, s, d = x.shape
    depth = w_ada.shape[0]
    assert s % PAIR_KEYS == 0 and d % LANES == 0
    mod = _mod_call(c, w_ada, b_ada).reshape(depth, b, 6, 1, d)
    cos_t, sa_t, sb_t = _rotary_tables(s)
    w_in, w_pool, w_out, w_up, w_down = (w.astype(BF16) for w in (w_in, w_pool, w_out, w_up, w_down))
    (g_pre_mix, pool_scale, pool_out_gain, attn_out_gain, g_post_mix, g_pre_ffn, conv_b,
     g_post_ffn) = (v[:, None, :] for v in (g_pre_mix, pool_scale, pool_out_gain, attn_out_gain,
                                            g_post_mix, g_pre_ffn, conv_b, g_post_ffn))
    for l in range(depth):
        q, k, vt, pn = _in_call(l, x, mod, g_pre_mix, w_in, cos_t, sa_t, sb_t, w_pool,
                                pool_scale, pool_out_gain)
        ao = _attn_call(q, k, vt)
        x = _ffn_call(l, ao, pn, x, mod, attn_out_gain, w_out, g_post_mix,
                      g_pre_ffn, w_up, conv_w, conv_b, w_down, g_post_ffn)
    return x
```

```python
import functools

import jax
import jax.numpy as jnp
import numpy as np
from jax import lax
from jax.experimental import pallas as pl
from jax.experimental.pallas import tpu as pltpu

F32 = jnp.float32
BF16 = jnp.bfloat16

HEAD_DIM = 64
HEADS_PER_GROUP = 2
LANES = 128
MOBA_BLOCK = 256
MOBA_TOP_K = 3
POOL_WINDOWS = (2, 4, 8, 16)
POOL_HALO = 16
ROT_DIM = HEAD_DIM // 4
ROPE_THETA = 500000.0
CONV_WIDTH = 3
CONV_HALO = 8
NORM_EPS = 1e-6
NEG_INF = -1e30
LOG2_E = 1.4426950408889634
VT_ROWS = HEAD_DIM + 16
PAIR_KEYS = 2 * MOBA_BLOCK
VMEM_LIMIT = 48 * 1024 * 1024
MOD_SH1, MOD_SC1, MOD_GT1, MOD_SH2, MOD_SC2, MOD_GT2 = range(6)

_NT = (((1,), (1,)), ((), ()))


def _rms(x):
    return x * lax.rsqrt(jnp.mean(x * x, axis=-1, keepdims=True) + NORM_EPS)


def _params(*sem):
    return pltpu.CompilerParams(dimension_semantics=sem, vmem_limit_bytes=VMEM_LIMIT)


def _mod_kernel(c_ref, w_ref, b_ref, o_ref):
    c = c_ref[...]
    c_act = c * jax.nn.sigmoid(c)
    o_ref[0] = jnp.dot(c_act.astype(BF16), w_ref[0].astype(BF16),
                       preferred_element_type=F32) + b_ref[0]


def _mod_call(c, w_ada, b_ada, tn=1536):
    nl, d, n6 = w_ada.shape
    b = c.shape[0]
    return pl.pallas_call(
        _mod_kernel,
        grid=(nl, n6 // tn),
        in_specs=[pl.BlockSpec((b, d), lambda l, j: (0, 0)),
                  pl.BlockSpec((1, d, tn), lambda l, j: (l, 0, j)),
                  pl.BlockSpec((1, 1, tn), lambda l, j: (l, 0, j))],
        out_specs=pl.BlockSpec((1, b, tn), lambda l, j: (l, 0, j)),
        out_shape=jax.ShapeDtypeStruct((nl, b, n6), F32),
        compiler_params=_params("arbitrary", "arbitrary"),
        name="adaln_mod",
    )(c, w_ada, b_ada.reshape(nl, 1, n6))


def _in_kernel(x_ref, sc_ref, sh_ref, g_ref, w_ref, cos_ref, sa_ref, sb_ref, wp_ref, ps_ref,
               pg_ref, q_ref, k_ref, vt_ref, pn_ref, halo_ref, *, ts, aw):
    si = pl.program_id(1)
    h = _rms(x_ref[0]) * (g_ref[...] * (1.0 + sc_ref[0])) + sh_ref[0]
    z = jnp.dot(h.astype(BF16), w_ref[...], preferred_element_type=F32)

    cos = cos_ref[...]
    sa = sa_ref[...]
    sb = sb_ref[...]

    def rot(t):
        return t * cos + pltpu.roll(t, LANES - ROT_DIM // 2, 1) * sa + pltpu.roll(t, ROT_DIM // 2, 1) * sb

    scale = HEAD_DIM ** -0.5 * LOG2_E
    ngroups = aw // LANES
    ones = jnp.ones((VT_ROWS - HEAD_DIM, PAIR_KEYS), BF16)
    for gi in range(ngroups):
        lo = gi * LANES
        q_ref[0, :, lo:lo + LANES] = (rot(z[:, lo:lo + LANES]) * scale).astype(BF16)
        k_ref[0, :, lo:lo + LANES] = rot(z[:, aw + lo:aw + lo + LANES]).astype(BF16)
        vt = z[:, 2 * aw + lo:2 * aw + lo + LANES].T.astype(BF16)
        for hh in range(HEADS_PER_GROUP):
            head = gi * HEADS_PER_GROUP + hh
            for jp in range(ts // PAIR_KEYS):
                vt_ref[0, head, jp, 0:HEAD_DIM, :] = vt[hh * HEAD_DIM:(hh + 1) * HEAD_DIM,
                                                        jp * PAIR_KEYS:(jp + 1) * PAIR_KEYS]
                vt_ref[0, head, jp, HEAD_DIM:VT_ROWS, :] = ones

    pz = z[:, 3 * aw:]
    prev = jnp.where(si == 0, 0.0, halo_ref[...])
    halo_ref[...] = pz[ts - POOL_HALO:, :]
    ext = jnp.concatenate([prev, pz], axis=0)
    tpos = si * ts + lax.broadcasted_iota(jnp.int32, (ts, 1), 0)
    outs = []
    for g, w in enumerate(POOL_WINDOWS):
        lo = g * LANES
        s = ext[:, lo:lo + LANES]
        span = 1
        while span < w:
            s = s[span:] + s[:-span]
            span *= 2
        s = s[POOL_HALO - (w - 1):POOL_HALO - (w - 1) + ts]
        cnt = jnp.minimum(tpos + 1, w).astype(F32)
        pooled = s / cnt - pz[:, lo:lo + LANES]
        outs.append(jnp.dot(pooled.astype(BF16), wp_ref[g], preferred_element_type=F32))
    po = jnp.concatenate(outs, axis=1) * ps_ref[...]
    pn_ref[0] = (_rms(po) * pg_ref[...]).astype(BF16)


def _mod_spec(l, chunk, d):
    return pl.BlockSpec((None, None, 1, 1, d), lambda bi, si: (l, bi, chunk, 0, 0))


def _layer_spec(l, arr, **kw):
    return pl.BlockSpec((None,) + arr.shape[1:], lambda bi, si: (l,) + (0,) * (arr.ndim - 1), **kw)


def _in_call(l, x, mod, g, w_in, cos_t, sa_t, sb_t, w_pool, pool_scale, pool_gain, ts=512):
    b, s, d = x.shape
    n_in = w_in.shape[2]
    pw = w_pool.shape[1] * w_pool.shape[2]
    aw = (n_in - pw) // 3
    nheads = aw // HEAD_DIM
    tok = lambda bi, si: (bi, si, 0)
    return pl.pallas_call(
        functools.partial(_in_kernel, ts=ts, aw=aw),
        grid=(b, s // ts),
        in_specs=[pl.BlockSpec((1, ts, d), tok),
                  _mod_spec(l, MOD_SC1, d),
                  _mod_spec(l, MOD_SH1, d),
                  _layer_spec(l, g),
                  _layer_spec(l, w_in),
                  pl.BlockSpec((ts, LANES), lambda bi, si: (si, 0)),
                  pl.BlockSpec((ts, LANES), lambda bi, si: (si, 0)),
                  pl.BlockSpec((ts, LANES), lambda bi, si: (si, 0)),
                  _layer_spec(l, w_pool),
                  _layer_spec(l, pool_scale),
                  _layer_spec(l, pool_gain)],
        out_specs=[pl.BlockSpec((1, ts, aw), tok),
                   pl.BlockSpec((1, ts, aw), tok),
                   pl.BlockSpec((1, nheads, ts // PAIR_KEYS, VT_ROWS, PAIR_KEYS),
                                lambda bi, si: (bi, 0, si, 0, 0)),
                   pl.BlockSpec((1, ts, pw), tok)],
        out_shape=[jax.ShapeDtypeStruct((b, s, aw), BF16),
                   jax.ShapeDtypeStruct((b, s, aw), BF16),
                   jax.ShapeDtypeStruct((b, nheads, s // PAIR_KEYS, VT_ROWS, PAIR_KEYS), BF16),
                   jax.ShapeDtypeStruct((b, s, pw), BF16)],
        scratch_shapes=[pltpu.VMEM((POOL_HALO, pw), F32)],
        compiler_params=_params("arbitrary", "arbitrary"),
        name="in_proj",
    )(x, mod, mod, g, w_in, cos_t, sa_t, sb_t, w_pool, pool_scale, pool_gain)


def _attn_kernel(q_ref, k_ref, vt_ref, o_ref, km_ref, bias_ref, s0_ref, s1_ref, *, nb):
    blk = MOBA_BLOCK
    for j in range(nb):
        kj = k_ref[0, j * blk:(j + 1) * blk, :].astype(F32)
        km_ref[j:j + 1, :] = jnp.mean(kj, axis=0, keepdims=True)
    s_slots = (s0_ref, s1_ref)

    def tile(t, prev_accs):
        _attn_write(jnp.maximum(t - 1, 0), prev_accs, o_ref)
        return _attn_tile(t, q_ref, k_ref, vt_ref, km_ref, bias_ref, s_slots, nb)

    ntiles = nb // 2
    dummy = tuple(jnp.ones((VT_ROWS, PAIR_KEYS), F32) for _ in range(HEADS_PER_GROUP))
    _attn_write(ntiles - 1, lax.fori_loop(0, ntiles, tile, dummy), o_ref)


def _attn_write(t, accs, o_ref):
    rows = pl.ds(pl.multiple_of(t * PAIR_KEYS, PAIR_KEYS), PAIR_KEYS)
    outs = [acc[0:HEAD_DIM, :] / acc[HEAD_DIM:HEAD_DIM + 1, :] for acc in accs]
    o_ref[0, rows, :] = jnp.concatenate(outs, axis=0).T


def _attn_tile(t, q_ref, k_ref, vt_ref, km_ref, bias_ref, s_slots, nb):
    blk = MOBA_BLOCK
    tq = PAIR_KEYS
    heads = range(HEADS_PER_GROUP)
    rows = pl.ds(pl.multiple_of(t * tq, tq), tq)
    q2 = q_ref[0, rows, :]
    lane = lax.broadcasted_iota(jnp.int32, (1, LANES), 1)
    qh = [jnp.where((lane >= hh * HEAD_DIM) & (lane < (hh + 1) * HEAD_DIM), q2, jnp.zeros_like(q2))
          for hh in heads]

    def pair_keys(pi):
        return k_ref[0, pl.ds(pl.multiple_of(pi * PAIR_KEYS, PAIR_KEYS), PAIR_KEYS), :]

    def issue_scores(j, slot):
        kj = k_ref[0, pl.ds(pl.multiple_of(j * blk, blk), blk), :]
        maxes = []
        for hh in heads:
            s = lax.dot_general(kj, qh[hh], _NT, preferred_element_type=F32)
            s_slots[slot][hh] = s
            maxes.append(jnp.max(s, axis=0, keepdims=True))
        return tuple(maxes)

    def softmax_pv(hh, s, cmax, j, vt_blk, state):
        b = bias_ref[hh, j]
        m_new = jnp.maximum(state[0], cmax + b)
        p = jnp.exp2(s - (m_new - b)).astype(BF16)
        acc = jnp.exp2(state[0] - m_new) * state[1] + jnp.dot(vt_blk, p, preferred_element_type=F32)
        return m_new, acc

    km = km_ref[...]
    km_hi = km.astype(BF16)
    km_lo = (km - km_hi.astype(F32)).astype(BF16)
    kown = pair_keys(t)
    lhs = jnp.concatenate([km_hi, km_lo, kown[0:blk]], axis=0)
    r = [lax.dot_general(lhs, qh[hh], _NT, preferred_element_type=F32) for hh in heads]
    r2 = [lax.dot_general(kown[blk:], qh[hh][blk:], _NT, preferred_element_type=F32) for hh in heads]
    maxes0 = issue_scores(0, 0)

    jidx = lax.broadcasted_iota(jnp.int32, (nb, 1), 0)
    qblk = 2 * t + lax.broadcasted_iota(jnp.int32, (1, tq), 1) // blk
    past = jidx < qblk
    for hh in heads:
        gate = jnp.where(past, r[hh][0:nb] + r[hh][nb:2 * nb], NEG_INF)
        picked = jnp.zeros(gate.shape, jnp.bool_)
        for _ in range(min(MOBA_TOP_K, nb)):
            top = jnp.max(gate, axis=0, keepdims=True)
            first = jnp.min(jnp.where(gate == top, jidx, nb), axis=0, keepdims=True)
            hit = jidx == first
            picked = picked | hit
            gate = jnp.where(hit, -jnp.inf, gate)
        bias = jnp.where(picked & past, 0.0, NEG_INF).astype(F32)
        for j in range(nb):
            bias_ref[hh, j] = bias[j:j + 1, :]

    causal = (lax.broadcasted_iota(jnp.int32, (blk, blk), 0)
              <= lax.broadcasted_iota(jnp.int32, (blk, blk), 1))
    own = []
    for hh in heads:
        vt_own = vt_ref[0, hh, t]
        s_a = jnp.where(causal, r[hh][2 * nb:, 0:blk], NEG_INF)
        m_a = jnp.max(s_a, axis=0, keepdims=True)
        acc_a = jnp.dot(vt_own[:, 0:blk], jnp.exp2(s_a - m_a).astype(BF16), preferred_element_type=F32)
        s_p = r[hh][2 * nb:, blk:]
        s_o = jnp.where(causal, r2[hh], NEG_INF)
        b_p = bias_ref[hh, 2 * t][:, blk:]
        m_b = jnp.maximum(jnp.max(s_p, axis=0, keepdims=True) + b_p, jnp.max(s_o, axis=0, keepdims=True))
        p_b = jnp.concatenate([jnp.exp2(s_p - (m_b - b_p)).astype(BF16),
                               jnp.exp2(s_o - m_b).astype(BF16)], axis=0)
        acc_b = jnp.dot(vt_own, p_b, preferred_element_type=F32)
        own.append((jnp.concatenate([m_a, m_b], axis=1), jnp.concatenate([acc_a, acc_b], axis=1)))

    def trip(j, carry, slot, vt_blocks):
        state, maxes = carry
        nxt = issue_scores(jnp.minimum(j + 1, nb - 1), 1 - slot)
        return tuple(softmax_pv(hh, s_slots[slot][hh], maxes[hh], j, vt_blocks[hh], state[hh])
                     for hh in heads), nxt

    def pair_trips(u, carry):
        vt_pair = [vt_ref[0, hh, u] for hh in heads]
        carry = trip(2 * u, carry, 0, [v[:, 0:blk] for v in vt_pair])
        return trip(2 * u + 1, carry, 1, [v[:, blk:] for v in vt_pair])

    def two_pairs(v, carry):
        return pair_trips(2 * v + 1, pair_trips(2 * v, carry))

    carry = lax.fori_loop(0, t // 2, two_pairs, (tuple(own), maxes0))
    carry = lax.cond(t % 2 == 1, lambda: pair_trips(t - 1, carry), lambda: carry)
    return tuple(acc for _, acc in carry[0])


def _attn_call(q, k, vt):
    b, s, aw = q.shape
    ngroups = aw // LANES
    nb = s // MOBA_BLOCK
    tq = PAIR_KEYS
    return pl.pallas_call(
        functools.partial(_attn_kernel, nb=nb),
        grid=(b, ngroups),
        in_specs=[pl.BlockSpec((1, s, LANES), lambda bi, gi: (bi, 0, gi)),
                  pl.BlockSpec((1, s, LANES), lambda bi, gi: (bi, 0, gi)),
                  pl.BlockSpec((1, HEADS_PER_GROUP, nb // 2, VT_ROWS, PAIR_KEYS),
                               lambda bi, gi: (bi, gi, 0, 0, 0))],
        out_specs=pl.BlockSpec((1, s, LANES), lambda bi, gi: (bi, 0, gi)),
        out_shape=jax.ShapeDtypeStruct((b, s, aw), F32),
        scratch_shapes=[pltpu.VMEM((nb, LANES), F32),
                        pltpu.VMEM((HEADS_PER_GROUP, nb, 1, tq), F32),
                        pltpu.VMEM((HEADS_PER_GROUP, MOBA_BLOCK, tq), F32),
                        pltpu.VMEM((HEADS_PER_GROUP, MOBA_BLOCK, tq), F32)],
        compiler_params=_params("arbitrary", "arbitrary"),
        name="moba_attn",
    )(q, k, vt)


def _ffn_kernel(ao_ref, pn_ref, xin_ref, ga_ref, wo_ref, gpm_ref, gt1_ref,
                sc_ref, sh_ref, g_ref, wu_ref, cw_ref, cb_ref, wd_ref, gp_ref, gt_ref,
                o_ref, tail_ref, act_ref, *, ts, tf):
    si = pl.program_id(1)
    aw = ao_ref.shape[2]
    an = (_rms(ao_ref[0]) * ga_ref[...]).astype(BF16)
    y = (jnp.dot(an, wo_ref[0:aw, :], preferred_element_type=F32)
         + jnp.dot(pn_ref[0], wo_ref[aw:, :], preferred_element_type=F32))
    x = xin_ref[0] + _rms(y) * (gt1_ref[0] * gpm_ref[...])

    hb = (_rms(x) * (g_ref[...] * (1.0 + sc_ref[0])) + sh_ref[0]).astype(BF16)
    d_ff = wd_ref.shape[0]
    first = si == 0

    def conv(lo):
        u = jnp.dot(hb, wu_ref[:, lo:lo + tf], preferred_element_type=F32)
        tail = jnp.where(first, 0.0, tail_ref[:, lo:lo + tf])
        tail_ref[:, lo:lo + tf] = u[ts - CONV_HALO:, :]
        ucat = jnp.concatenate([tail, u], axis=0)
        uc = cb_ref[:, lo:lo + tf] + cw_ref[CONV_WIDTH - 1:CONV_WIDTH, lo:lo + tf] * u
        for j in range(CONV_WIDTH - 1):
            back = CONV_WIDTH - 1 - j
            uc = uc + cw_ref[j:j + 1, lo:lo + tf] * ucat[CONV_HALO - back:CONV_HALO - back + ts, :]
        return uc

    for fi in range(d_ff // tf):
        a = conv(fi * tf)
        g = conv(d_ff + fi * tf)
        act_ref[:, fi * tf:(fi + 1) * tf] = (a * jax.nn.sigmoid(a) * g).astype(BF16)
    y = jnp.dot(act_ref[...], wd_ref[...], preferred_element_type=F32)
    o_ref[0] = x + _rms(y) * (gt_ref[0] * gp_ref[...])


def _ffn_call(l, ao, pn, x, mod, attn_gain, w_out, g_post_mix,
              g_pre, w_up, conv_w, conv_b, w_down, g_post, ts=512, tf=256):
    b, s, d = x.shape
    f = w_down.shape[1]
    aw = ao.shape[2]
    pw = pn.shape[2]
    tok = lambda bi, si: (bi, si, 0)
    resident = dict(pipeline_mode=pl.Buffered(1))
    return pl.pallas_call(
        functools.partial(_ffn_kernel, ts=ts, tf=tf),
        grid=(b, s // ts),
        in_specs=[pl.BlockSpec((1, ts, aw), tok),
                  pl.BlockSpec((1, ts, pw), tok),
                  pl.BlockSpec((1, ts, d), tok),
                  _layer_spec(l, attn_gain),
                  _layer_spec(l, w_out, **resident),
                  _layer_spec(l, g_post_mix),
                  _mod_spec(l, MOD_GT1, d),
                  _mod_spec(l, MOD_SC2, d),
                  _mod_spec(l, MOD_SH2, d),
                  _layer_spec(l, g_pre),
                  _layer_spec(l, w_up, **resident),
                  _layer_spec(l, conv_w),
                  _layer_spec(l, conv_b),
                  _layer_spec(l, w_down, **resident),
                  _layer_spec(l, g_post),
                  _mod_spec(l, MOD_GT2, d)],
        out_specs=pl.BlockSpec((1, ts, d), tok),
        out_shape=jax.ShapeDtypeStruct((b, s, d), F32),
        scratch_shapes=[pltpu.VMEM((CONV_HALO, 2 * f), F32),
                        pltpu.VMEM((ts, f), BF16)],
        compiler_params=_params("arbitrary", "arbitrary"),
        name="conv_ffn",
    )(ao, pn, x, attn_gain, w_out, g_post_mix, mod,
      mod, mod, g_pre, w_up, conv_w, conv_b, w_down, g_post, mod)


def _rotary_tables(s):
    half = ROT_DIM // 2
    pos = jnp.arange(s, dtype=F32)
    inv_freq = jnp.power(jnp.float32(ROPE_THETA), -jnp.arange(0, ROT_DIM, 2, dtype=F32) / ROT_DIM)
    dim = np.arange(LANES) % HEAD_DIM
    freq = jnp.where(dim < ROT_DIM, inv_freq[dim % half], 0.0)
    ang = pos[:, None] * freq[None, :]
    cos = jnp.cos(ang)
    sin = jnp.sin(ang)
    sa = jnp.where(dim < half, -sin, 0.0)
    sb = jnp.where((dim >= half) & (dim < ROT_DIM), sin, 0.0)
    return cos, sa, sb


def kernel(x, c, w_ada, b_ada, g_pre_mix, w_in, w_pool, pool_scale, attn_out_gain, pool_out_gain,
           w_out, g_post_mix, g_pre_ffn, w_up, conv_w, conv_b, w_down, g_post_ffn):
    b, s, d = x.shape
    depth = w_ada.shape[0]
    assert s % PAIR_KEYS == 0 and d % LANES == 0
    mod = _mod_call(c, w_ada, b_ada).reshape(depth, b, 6, 1, d)
    cos_t, sa_t, sb_t = _rotary_tables(s)
    w_in, w_pool, w_out, w_up, w_down = (w.astype(BF16) for w in (w_in, w_pool, w_out, w_up, w_down))
    (g_pre_mix, pool_scale, pool_out_gain, attn_out_gain, g_post_mix, g_pre_ffn, conv_b,
     g_post_ffn) = (v[:, None, :] for v in (g_pre_mix, pool_scale, pool_out_gain, attn_out_gain,
                                            g_post_mix, g_pre_ffn, conv_b, g_post_ffn))
    for l in range(depth):
        q, k, vt, pn = _in_call(l, x, mod, g_pre_mix, w_in, cos_t, sa_t, sb_t, w_pool,
                                pool_scale, pool_out_gain)
        ao = _attn_call(q, k, vt)
        x = _ffn_call(l, ao, pn, x, mod, attn_out_gain, w_out, g_post_mix,
                      g_pre_ffn, w_up, conv_w, conv_b, w_down, g_post_ffn)
    return x
```

```python
import functools

import jax
import jax.numpy as jnp
import numpy as np
from jax import lax
from jax.experimental import pallas as pl
from jax.experimental.pallas import tpu as pltpu

F32 = jnp.float32
BF16 = jnp.bfloat16

HEAD_DIM = 64
HEADS_PER_GROUP = 2
LANES = 128
MXU_COLS = 256
MOBA_BLOCK = 256
MOBA_TOP_K = 3
POOL_WINDOWS = (2, 4, 8, 16)
POOL_HALO = 16
ROT_DIM = HEAD_DIM // 4
ROPE_THETA = 500000.0
CONV_WIDTH = 3
CONV_HALO = 8
NORM_EPS = 1e-6
NEG_INF = -1e30
LOG2_E = 1.4426950408889634
VT_ROWS = HEAD_DIM + 16
PAIR_KEYS = 2 * MOBA_BLOCK
VMEM_LIMIT = 48 * 1024 * 1024
MOD_SH1, MOD_SC1, MOD_GT1, MOD_SH2, MOD_SC2, MOD_GT2 = range(6)

_NT = (((1,), (1,)), ((), ()))


def _rms(x):
    return x * lax.rsqrt(jnp.mean(x * x, axis=-1, keepdims=True) + NORM_EPS)


def _params(*sem):
    return pltpu.CompilerParams(dimension_semantics=sem, vmem_limit_bytes=VMEM_LIMIT)


def _mod_kernel(c_ref, w_ref, b_ref, o_ref):
    c = c_ref[...]
    c_act = c * jax.nn.sigmoid(c)
    o_ref[0] = jnp.dot(c_act.astype(BF16), w_ref[0].astype(BF16),
                       preferred_element_type=F32) + b_ref[0]


def _mod_call(c, w_ada, b_ada, tn=1536):
    nl, d, n6 = w_ada.shape
    b = c.shape[0]
    return pl.pallas_call(
        _mod_kernel,
        grid=(nl, n6 // tn),
        in_specs=[pl.BlockSpec((b, d), lambda l, j: (0, 0)),
                  pl.BlockSpec((1, d, tn), lambda l, j: (l, 0, j)),
                  pl.BlockSpec((1, 1, tn), lambda l, j: (l, 0, j))],
        out_specs=pl.BlockSpec((1, b, tn), lambda l, j: (l, 0, j)),
        out_shape=jax.ShapeDtypeStruct((nl, b, n6), F32),
        compiler_params=_params("arbitrary", "arbitrary"),
        name="adaln_mod",
    )(c, w_ada, b_ada.reshape(nl, 1, n6))


def _in_kernel(x_ref, sc_ref, sh_ref, g_ref, w_ref, cos_ref, sa_ref, sb_ref, wp_ref, ps_ref,
               pg_ref, q_ref, k_ref, vt_ref, pn_ref, halo_ref, *, ts, aw):
    si = pl.program_id(1)
    h = _rms(x_ref[0]) * (g_ref[...] * (1.0 + sc_ref[0])) + sh_ref[0]
    hb = h.astype(BF16)
    def proj(lo, hi):
        return jnp.concatenate([jnp.dot(hb, w_ref[:, c:c + MXU_COLS], preferred_element_type=F32)
                                for c in range(lo, hi, MXU_COLS)], axis=1)

    pz = proj(3 * aw, w_ref.shape[1])
    z = proj(0, 3 * aw)

    cos = cos_ref[...]
    sa = sa_ref[...]
    sb = sb_ref[...]

    def rot(t):
        return t * cos + pltpu.roll(t, LANES - ROT_DIM // 2, 1) * sa + pltpu.roll(t, ROT_DIM // 2, 1) * sb

    scale = HEAD_DIM ** -0.5 * LOG2_E
    ngroups = aw // LANES
    ones = jnp.ones((VT_ROWS - HEAD_DIM, PAIR_KEYS), BF16)
    for gi in range(ngroups):
        lo = gi * LANES
        q_ref[0, :, lo:lo + LANES] = (rot(z[:, lo:lo + LANES]) * scale).astype(BF16)
        k_ref[0, :, lo:lo + LANES] = rot(z[:, aw + lo:aw + lo + LANES]).astype(BF16)
        vt = z[:, 2 * aw + lo:2 * aw + lo + LANES].T.astype(BF16)
        for hh in range(HEADS_PER_GROUP):
            head = gi * HEADS_PER_GROUP + hh
            for jp in range(ts // PAIR_KEYS):
                vt_ref[0, head, jp, 0:HEAD_DIM, :] = vt[hh * HEAD_DIM:(hh + 1) * HEAD_DIM,
                                                        jp * PAIR_KEYS:(jp + 1) * PAIR_KEYS]
                vt_ref[0, head, jp, HEAD_DIM:VT_ROWS, :] = ones

    prev = jnp.where(si == 0, 0.0, halo_ref[...])
    halo_ref[...] = pz[ts - POOL_HALO:, :]
    ext = jnp.concatenate([prev, pz], axis=0)
    tpos = si * ts + lax.broadcasted_iota(jnp.int32, (ts, 1), 0)
    outs = []
    for g, w in enumerate(POOL_WINDOWS):
        lo = g * LANES
        s = ext[:, lo:lo + LANES]
        span = 1
        while span < w:
            s = s[span:] + s[:-span]
            span *= 2
        s = s[POOL_HALO - (w - 1):POOL_HALO - (w - 1) + ts]
        cnt = jnp.minimum(tpos + 1, w).astype(F32)
        pooled = s / cnt - pz[:, lo:lo + LANES]
        outs.append(jnp.dot(pooled.astype(BF16), wp_ref[g], preferred_element_type=F32))
    po = jnp.concatenate(outs, axis=1) * ps_ref[...]
    pn_ref[0] = (_rms(po) * pg_ref[...]).astype(BF16)


def _mod_spec(l, chunk, d):
    return pl.BlockSpec((None, None, 1, 1, d), lambda bi, si: (l, bi, chunk, 0, 0))


def _layer_spec(l, arr, **kw):
    return pl.BlockSpec((None,) + arr.shape[1:], lambda bi, si: (l,) + (0,) * (arr.ndim - 1), **kw)


def _in_call(l, x, mod, g, w_in, cos_t, sa_t, sb_t, w_pool, pool_scale, pool_gain, ts=512):
    b, s, d = x.shape
    n_in = w_in.shape[2]
    pw = w_pool.shape[1] * w_pool.shape[2]
    aw = (n_in - pw) // 3
    nheads = aw // HEAD_DIM
    tok = lambda bi, si: (bi, si, 0)
    return pl.pallas_call(
        functools.partial(_in_kernel, ts=ts, aw=aw),
        grid=(b, s // ts),
        in_specs=[pl.BlockSpec((1, ts, d), tok),
                  _mod_spec(l, MOD_SC1, d),
                  _mod_spec(l, MOD_SH1, d),
                  _layer_spec(l, g),
                  _layer_spec(l, w_in),
                  pl.BlockSpec((ts, LANES), lambda bi, si: (si, 0)),
                  pl.BlockSpec((ts, LANES), lambda bi, si: (si, 0)),
                  pl.BlockSpec((ts, LANES), lambda bi, si: (si, 0)),
                  _layer_spec(l, w_pool),
                  _layer_spec(l, pool_scale),
                  _layer_spec(l, pool_gain)],
        out_specs=[pl.BlockSpec((1, ts, aw), tok),
                   pl.BlockSpec((1, ts, aw), tok),
                   pl.BlockSpec((1, nheads, ts // PAIR_KEYS, VT_ROWS, PAIR_KEYS),
                                lambda bi, si: (bi, 0, si, 0, 0)),
                   pl.BlockSpec((1, ts, pw), tok)],
        out_shape=[jax.ShapeDtypeStruct((b, s, aw), BF16),
                   jax.ShapeDtypeStruct((b, s, aw), BF16),
                   jax.ShapeDtypeStruct((b, nheads, s // PAIR_KEYS, VT_ROWS, PAIR_KEYS), BF16),
                   jax.ShapeDtypeStruct((b, s, pw), BF16)],
        scratch_shapes=[pltpu.VMEM((POOL_HALO, pw), F32)],
        compiler_params=_params("arbitrary", "arbitrary"),
        name="in_proj",
    )(x, mod, mod, g, w_in, cos_t, sa_t, sb_t, w_pool, pool_scale, pool_gain)


def _attn_kernel(q_ref, k_ref, vt_ref, o_ref, km_ref, bias_ref, s0_ref, s1_ref, *, nb):
    blk = MOBA_BLOCK
    for j in range(nb):
        kj = k_ref[0, j * blk:(j + 1) * blk, :].astype(F32)
        km_ref[j:j + 1, :] = jnp.mean(kj, axis=0, keepdims=True)
    s_slots = (s0_ref, s1_ref)

    def tile(t, prev_accs):
        _attn_write(jnp.maximum(t - 1, 0), prev_accs, o_ref)
        return _attn_tile(t, q_ref, k_ref, vt_ref, km_ref, bias_ref, s_slots, nb)

    ntiles = nb // 2
    dummy = tuple(jnp.ones((VT_ROWS, PAIR_KEYS), F32) for _ in range(HEADS_PER_GROUP))
    _attn_write(ntiles - 1, lax.fori_loop(0, ntiles, tile, dummy), o_ref)


def _attn_write(t, accs, o_ref):
    rows = pl.ds(pl.multiple_of(t * PAIR_KEYS, PAIR_KEYS), PAIR_KEYS)
    outs = [acc[0:HEAD_DIM, :] / acc[HEAD_DIM:HEAD_DIM + 1, :] for acc in accs]
    o_ref[0, rows, :] = jnp.concatenate(outs, axis=0).T


def _attn_tile(t, q_ref, k_ref, vt_ref, km_ref, bias_ref, s_slots, nb):
    blk = MOBA_BLOCK
    tq = PAIR_KEYS
    heads = range(HEADS_PER_GROUP)
    rows = pl.ds(pl.multiple_of(t * tq, tq), tq)
    q2 = q_ref[0, rows, :]
    lane = lax.broadcasted_iota(jnp.int32, (1, LANES), 1)
    qh = [jnp.where((lane >= hh * HEAD_DIM) & (lane < (hh + 1) * HEAD_DIM), q2, jnp.zeros_like(q2))
          for hh in heads]

    def pair_keys(pi):
        return k_ref[0, pl.ds(pl.multiple_of(pi * PAIR_KEYS, PAIR_KEYS), PAIR_KEYS), :]

    def issue_scores(j, slot):
        kj = k_ref[0, pl.ds(pl.multiple_of(j * blk, blk), blk), :]
        maxes = []
        for hh in heads:
            s = lax.dot_general(kj, qh[hh], _NT, preferred_element_type=F32)
            s_slots[slot][hh] = s
            maxes.append(jnp.max(s, axis=0, keepdims=True))
        return tuple(maxes)

    def softmax_pv(hh, s, cmax, j, vt_blk, state):
        b = bias_ref[hh, j]
        m_new = jnp.maximum(state[0], cmax + b)
        p = jnp.exp2(s - (m_new - b)).astype(BF16)
        acc = jnp.exp2(state[0] - m_new) * state[1] + jnp.dot(vt_blk, p, preferred_element_type=F32)
        return m_new, acc

    km = km_ref[...]
    km_hi = km.astype(BF16)
    km_lo = (km - km_hi.astype(F32)).astype(BF16)
    kown = pair_keys(t)
    lhs = jnp.concatenate([km_hi, km_lo, kown[0:blk]], axis=0)
    r = [lax.dot_general(lhs, qh[hh], _NT, preferred_element_type=F32) for hh in heads]
    r2 = [lax.dot_general(kown[blk:], qh[hh][blk:], _NT, preferred_element_type=F32) for hh in heads]
    maxes0 = issue_scores(0, 0)

    jidx = lax.broadcasted_iota(jnp.int32, (nb, 1), 0)
    qblk = 2 * t + lax.broadcasted_iota(jnp.int32, (1, tq), 1) // blk
    past = jidx < qblk
    for hh in heads:
        gate = jnp.where(past, r[hh][0:nb] + r[hh][nb:2 * nb], NEG_INF)
        picked = jnp.zeros(gate.shape, jnp.bool_)
        for _ in range(min(MOBA_TOP_K, nb)):
            top = jnp.max(gate, axis=0, keepdims=True)
            first = jnp.min(jnp.where(gate == top, jidx, nb), axis=0, keepdims=True)
            hit = jidx == first
            picked = picked | hit
            gate = jnp.where(hit, -jnp.inf, gate)
        bias = jnp.where(picked & past, 0.0, NEG_INF).astype(F32)
        for j in range(nb):
            bias_ref[hh, j] = bias[j:j + 1, :]

    causal = (lax.broadcasted_iota(jnp.int32, (blk, blk), 0)
              <= lax.broadcasted_iota(jnp.int32, (blk, blk), 1))
    own = []
    for hh in heads:
        vt_own = vt_ref[0, hh, t]
        s_a = jnp.where(causal, r[hh][2 * nb:, 0:blk], NEG_INF)
        m_a = jnp.max(s_a, axis=0, keepdims=True)
        acc_a = jnp.dot(vt_own[:, 0:blk], jnp.exp2(s_a - m_a).astype(BF16), preferred_element_type=F32)
        s_p = r[hh][2 * nb:, blk:]
        s_o = jnp.where(causal, r2[hh], NEG_INF)
        b_p = bias_ref[hh, 2 * t][:, blk:]
        m_b = jnp.maximum(jnp.max(s_p, axis=0, keepdims=True) + b_p, jnp.max(s_o, axis=0, keepdims=True))
        p_b = jnp.concatenate([jnp.exp2(s_p - (m_b - b_p)).astype(BF16),
                               jnp.exp2(s_o - m_b).astype(BF16)], axis=0)
        acc_b = jnp.dot(vt_own, p_b, preferred_element_type=F32)
        own.append((jnp.concatenate([m_a, m_b], axis=1), jnp.concatenate([acc_a, acc_b], axis=1)))

    def trip(j, carry, slot, vt_blocks):
        state, maxes = carry
        nxt = issue_scores(jnp.minimum(j + 1, nb - 1), 1 - slot)
        return tuple(softmax_pv(hh, s_slots[slot][hh], maxes[hh], j, vt_blocks[hh], state[hh])
                     for hh in heads), nxt

    def pair_trips(u, carry):
        vt_pair = [vt_ref[0, hh, u] for hh in heads]
        carry = trip(2 * u, carry, 0, [v[:, 0:blk] for v in vt_pair])
        return trip(2 * u + 1, carry, 1, [v[:, blk:] for v in vt_pair])

    def two_pairs(v, carry):
        return pair_trips(2 * v + 1, pair_trips(2 * v, carry))

    carry = lax.fori_loop(0, t // 2, two_pairs, (tuple(own), maxes0))
    carry = lax.cond(t % 2 == 1, lambda: pair_trips(t - 1, carry), lambda: carry)
    return tuple(acc for _, acc in carry[0])


def _attn_call(q, k, vt):
    b, s, aw = q.shape
    ngroups = aw // LANES
    nb = s // MOBA_BLOCK
    tq = PAIR_KEYS
    return pl.pallas_call(
        functools.partial(_attn_kernel, nb=nb),
        grid=(b, ngroups),
        in_specs=[pl.BlockSpec((1, s, LANES), lambda bi, gi: (bi, 0, gi)),
                  pl.BlockSpec((1, s, LANES), lambda bi, gi: (bi, 0, gi)),
                  pl.BlockSpec((1, HEADS_PER_GROUP, nb // 2, VT_ROWS, PAIR_KEYS),
                               lambda bi, gi: (bi, gi, 0, 0, 0))],
        out_specs=pl.BlockSpec((1, s, LANES), lambda bi, gi: (bi, 0, gi)),
        out_shape=jax.ShapeDtypeStruct((b, s, aw), F32),
        scratch_shapes=[pltpu.VMEM((nb, LANES), F32),
                        pltpu.VMEM((HEADS_PER_GROUP, nb, 1, tq), F32),
                        pltpu.VMEM((HEADS_PER_GROUP, MOBA_BLOCK, tq), F32),
                        pltpu.VMEM((HEADS_PER_GROUP, MOBA_BLOCK, tq), F32)],
        compiler_params=_params("arbitrary", "arbitrary"),
        name="moba_attn",
    )(q, k, vt)


def _ffn_kernel(ao_ref, pn_ref, xin_ref, ga_ref, wo_ref, gpm_ref, gt1_ref,
                sc_ref, sh_ref, g_ref, wu_ref, cw_ref, cb_ref, wd_ref, gp_ref, gt_ref,
                o_ref, tail_ref, act_ref, *, ts, tf):
    si = pl.program_id(1)
    aw = ao_ref.shape[2]
    an = (_rms(ao_ref[0]) * ga_ref[...]).astype(BF16)
    y = (jnp.dot(an, wo_ref[0:aw, :], preferred_element_type=F32)
         + jnp.dot(pn_ref[0], wo_ref[aw:, :], preferred_element_type=F32))
    x = xin_ref[0] + _rms(y) * (gt1_ref[0] * gpm_ref[...])

    hb = (_rms(x) * (g_ref[...] * (1.0 + sc_ref[0])) + sh_ref[0]).astype(BF16)
    d_ff = wd_ref.shape[0]
    first = si == 0

    def conv(lo):
        u = jnp.dot(hb, wu_ref[:, lo:lo + tf], preferred_element_type=F32)
        tail = jnp.where(first, 0.0, tail_ref[:, lo:lo + tf])
        tail_ref[:, lo:lo + tf] = u[ts - CONV_HALO:, :]
        ucat = jnp.concatenate([tail, u], axis=0)
        uc = cb_ref[:, lo:lo + tf] + cw_ref[CONV_WIDTH - 1:CONV_WIDTH, lo:lo + tf] * u
        for j in range(CONV_WIDTH - 1):
            back = CONV_WIDTH - 1 - j
            uc = uc + cw_ref[j:j + 1, lo:lo + tf] * ucat[CONV_HALO - back:CONV_HALO - back + ts, :]
        return uc

    for fi in range(d_ff // tf):
        a = conv(fi * tf)
        g = conv(d_ff + fi * tf)
        act_ref[:, fi * tf:(fi + 1) * tf] = (a * jax.nn.sigmoid(a) * g).astype(BF16)
    y = jnp.dot(act_ref[...], wd_ref[...], preferred_element_type=F32)
    o_ref[0] = x + _rms(y) * (gt_ref[0] * gp_ref[...])


def _ffn_call(l, ao, pn, x, mod, attn_gain, w_out, g_post_mix,
              g_pre, w_up, conv_w, conv_b, w_down, g_post, ts=512, tf=256):
    b, s, d = x.shape
    f = w_down.shape[1]
    aw = ao.shape[2]
    pw = pn.shape[2]
    tok = lambda bi, si: (bi, si, 0)
    resident = dict(pipeline_mode=pl.Buffered(1))
    return pl.pallas_call(
        functools.partial(_ffn_kernel, ts=ts, tf=tf),
        grid=(b, s // ts),
        in_specs=[pl.BlockSpec((1, ts, aw), tok),
                  pl.BlockSpec((1, ts, pw), tok),
                  pl.BlockSpec((1, ts, d), tok),
                  _layer_spec(l, attn_gain),
                  _layer_spec(l, w_out, **resident),
                  _layer_spec(l, g_post_mix),
                  _mod_spec(l, MOD_GT1, d),
                  _mod_spec(l, MOD_SC2, d),
                  _mod_spec(l, MOD_SH2, d),
                  _layer_spec(l, g_pre),
                  _layer_spec(l, w_up, **resident),
                  _layer_spec(l, conv_w),
                  _layer_spec(l, conv_b),
                  _layer_spec(l, w_down, **resident),
                  _layer_spec(l, g_post),
                  _mod_spec(l, MOD_GT2, d)],
        out_specs=pl.BlockSpec((1, ts, d), tok),
        out_shape=jax.ShapeDtypeStruct((b, s, d), F32),
        scratch_shapes=[pltpu.VMEM((CONV_HALO, 2 * f), F32),
                        pltpu.VMEM((ts, f), BF16)],
        compiler_params=_params("arbitrary", "arbitrary"),
        name="conv_ffn",
    )(ao, pn, x, attn_gain, w_out, g_post_mix, mod,
      mod, mod, g_pre, w_up, conv_w, conv_b, w_down, g_post, mod)


def _rotary_tables(s):
    half = ROT_DIM // 2
    pos = jnp.arange(s, dtype=F32)
    inv_freq = jnp.power(jnp.float32(ROPE_THETA), -jnp.arange(0, ROT_DIM, 2, dtype=F32) / ROT_DIM)
    dim = np.arange(LANES) % HEAD_DIM
    freq = jnp.where(dim < ROT_DIM, inv_freq[dim % half], 0.0)
    ang = pos[:, None] * freq[None, :]
    cos = jnp.cos(ang)
    sin = jnp.sin(ang)
    sa = jnp.where(dim < half, -sin, 0.0)
    sb = jnp.where((dim >= half) & (dim < ROT_DIM), sin, 0.0)
    return cos, sa, sb


def kernel(x, c, w_ada, b_ada, g_pre_mix, w_in, w_pool, pool_scale, attn_out_gain, pool_out_gain,
           w_out, g_post_mix, g_pre_ffn, w_up, conv_w, conv_b, w_down, g_post_ffn):
    b, s, d = x.shape
    depth = w_ada.shape[0]
    assert s % PAIR_KEYS == 0 and d % LANES == 0
    mod = _mod_call(c, w_ada, b_ada).reshape(depth, b, 6, 1, d)
    cos_t, sa_t, sb_t = _rotary_tables(s)
    w_in, w_pool, w_out, w_up, w_down = (w.astype(BF16) for w in (w_in, w_pool, w_out, w_up, w_down))
    (g_pre_mix, pool_scale, pool_out_gain, attn_out_gain, g_post_mix, g_pre_ffn, conv_b,
     g_post_ffn) = (v[:, None, :] for v in (g_pre_mix, pool_scale, pool_out_gain, attn_out_gain,
                                            g_post_mix, g_pre_ffn, conv_b, g_post_ffn))
    for l in range(depth):
        q, k, vt, pn = _in_call(l, x, mod, g_pre_mix, w_in, cos_t, sa_t, sb_t, w_pool,
                                pool_scale, pool_out_gain)
        ao = _attn_call(q, k, vt)
        x = _ffn_call(l, ao, pn, x, mod, attn_out_gain, w_out, g_post_mix,
                      g_pre_ffn, w_up, conv_w, conv_b, w_down, g_post_ffn)
    return x
```

```python
import functools

import jax
import jax.numpy as jnp
import numpy as np
from jax import lax
from jax.experimental import pallas as pl
from jax.experimental.pallas import tpu as pltpu

F32 = jnp.float32
BF16 = jnp.bfloat16

HEAD_DIM = 64
HEADS_PER_GROUP = 2
LANES = 128
MXU_COLS = 256
MOBA_BLOCK = 256
MOBA_TOP_K = 3
POOL_WINDOWS = (2, 4, 8, 16)
POOL_HALO = 16
ROT_DIM = HEAD_DIM // 4
ROPE_THETA = 500000.0
CONV_WIDTH = 3
CONV_HALO = 8
NORM_EPS = 1e-6
NEG_INF = -1e30
LOG2_E = 1.4426950408889634
VT_ROWS = HEAD_DIM + 16
PAIR_KEYS = 2 * MOBA_BLOCK
VMEM_LIMIT = 56 * 1024 * 1024
MOD_SH1, MOD_SC1, MOD_GT1, MOD_SH2, MOD_SC2, MOD_GT2 = range(6)

_NT = (((1,), (1,)), ((), ()))


def _rms(x):
    return x * lax.rsqrt(jnp.mean(x * x, axis=-1, keepdims=True) + NORM_EPS)


def _params(*sem):
    return pltpu.CompilerParams(dimension_semantics=sem, vmem_limit_bytes=VMEM_LIMIT)


def _mod_kernel(c_ref, w_ref, b_ref, o_ref):
    c = c_ref[...]
    c_act = c * jax.nn.sigmoid(c)
    o_ref[0] = jnp.dot(c_act.astype(BF16), w_ref[0].astype(BF16),
                       preferred_element_type=F32) + b_ref[0]


def _mod_call(c, w_ada, b_ada, tn=1536):
    nl, d, n6 = w_ada.shape
    b = c.shape[0]
    return pl.pallas_call(
        _mod_kernel,
        grid=(nl, n6 // tn),
        in_specs=[pl.BlockSpec((b, d), lambda l, j: (0, 0)),
                  pl.BlockSpec((1, d, tn), lambda l, j: (l, 0, j)),
                  pl.BlockSpec((1, 1, tn), lambda l, j: (l, 0, j))],
        out_specs=pl.BlockSpec((1, b, tn), lambda l, j: (l, 0, j)),
        out_shape=jax.ShapeDtypeStruct((nl, b, n6), F32),
        compiler_params=_params("arbitrary", "arbitrary"),
        name="adaln_mod",
    )(c, w_ada, b_ada.reshape(nl, 1, n6))


def _in_kernel(x_ref, sc_ref, sh_ref, g_ref, w_ref, cos_ref, sa_ref, sb_ref, wp_ref, ps_ref,
               pg_ref, q_ref, k_ref, vt_ref, pn_ref, halo_ref, *, ts, aw):
    si = pl.program_id(1)
    h = _rms(x_ref[0]) * (g_ref[...] * (1.0 + sc_ref[0])) + sh_ref[0]
    hb = h.astype(BF16)
    def proj(lo, hi):
        return jnp.concatenate([jnp.dot(hb, w_ref[:, c:c + MXU_COLS], preferred_element_type=F32)
                                for c in range(lo, hi, MXU_COLS)], axis=1)

    pz = proj(3 * aw, w_ref.shape[1])
    z = proj(0, 3 * aw)

    cos = cos_ref[...]
    sa = sa_ref[...]
    sb = sb_ref[...]

    def rot(t):
        return t * cos + pltpu.roll(t, LANES - ROT_DIM // 2, 1) * sa + pltpu.roll(t, ROT_DIM // 2, 1) * sb

    scale = HEAD_DIM ** -0.5 * LOG2_E
    ngroups = aw // LANES
    ones = jnp.ones((VT_ROWS - HEAD_DIM, PAIR_KEYS), BF16)
    for gi in range(ngroups):
        lo = gi * LANES
        q_ref[0, :, lo:lo + LANES] = (rot(z[:, lo:lo + LANES]) * scale).astype(BF16)
        k_ref[0, :, lo:lo + LANES] = rot(z[:, aw + lo:aw + lo + LANES]).astype(BF16)
        vt = z[:, 2 * aw + lo:2 * aw + lo + LANES].T.astype(BF16)
        for hh in range(HEADS_PER_GROUP):
            head = gi * HEADS_PER_GROUP + hh
            for jp in range(ts // PAIR_KEYS):
                vt_ref[0, head, jp, 0:HEAD_DIM, :] = vt[hh * HEAD_DIM:(hh + 1) * HEAD_DIM,
                                                        jp * PAIR_KEYS:(jp + 1) * PAIR_KEYS]
                vt_ref[0, head, jp, HEAD_DIM:VT_ROWS, :] = ones

    prev = jnp.where(si == 0, 0.0, halo_ref[...])
    halo_ref[...] = pz[ts - POOL_HALO:, :]
    ext = jnp.concatenate([prev, pz], axis=0)
    tpos = si * ts + lax.broadcasted_iota(jnp.int32, (ts, 1), 0)
    outs = []
    for g, w in enumerate(POOL_WINDOWS):
        lo = g * LANES
        s = ext[:, lo:lo + LANES]
        span = 1
        while span < w:
            s = s[span:] + s[:-span]
            span *= 2
        s = s[POOL_HALO - (w - 1):POOL_HALO - (w - 1) + ts]
        cnt = jnp.minimum(tpos + 1, w).astype(F32)
        pooled = s / cnt - pz[:, lo:lo + LANES]
        outs.append(jnp.dot(pooled.astype(BF16), wp_ref[g], preferred_element_type=F32))
    po = jnp.concatenate(outs, axis=1) * ps_ref[...]
    pn_ref[0] = (_rms(po) * pg_ref[...]).astype(BF16)


def _mod_spec(l, chunk, d):
    return pl.BlockSpec((None, None, 1, 1, d), lambda bi, si: (l, bi, chunk, 0, 0))


def _layer_spec(l, arr, **kw):
    return pl.BlockSpec((None,) + arr.shape[1:], lambda bi, si: (l,) + (0,) * (arr.ndim - 1), **kw)


def _in_call(l, x, mod, g, w_in, cos_t, sa_t, sb_t, w_pool, pool_scale, pool_gain, ts=1024):
    b, s, d = x.shape
    n_in = w_in.shape[2]
    pw = w_pool.shape[1] * w_pool.shape[2]
    aw = (n_in - pw) // 3
    nheads = aw // HEAD_DIM
    tok = lambda bi, si: (bi, si, 0)
    return pl.pallas_call(
        functools.partial(_in_kernel, ts=ts, aw=aw),
        grid=(b, s // ts),
        in_specs=[pl.BlockSpec((1, ts, d), tok),
                  _mod_spec(l, MOD_SC1, d),
                  _mod_spec(l, MOD_SH1, d),
                  _layer_spec(l, g),
                  _layer_spec(l, w_in),
                  pl.BlockSpec((ts, LANES), lambda bi, si: (si, 0)),
                  pl.BlockSpec((ts, LANES), lambda bi, si: (si, 0)),
                  pl.BlockSpec((ts, LANES), lambda bi, si: (si, 0)),
                  _layer_spec(l, w_pool),
                  _layer_spec(l, pool_scale),
                  _layer_spec(l, pool_gain)],
        out_specs=[pl.BlockSpec((1, ts, aw), tok),
                   pl.BlockSpec((1, ts, aw), tok),
                   pl.BlockSpec((1, nheads, ts // PAIR_KEYS, VT_ROWS, PAIR_KEYS),
                                lambda bi, si: (bi, 0, si, 0, 0)),
                   pl.BlockSpec((1, ts, pw), tok)],
        out_shape=[jax.ShapeDtypeStruct((b, s, aw), BF16),
                   jax.ShapeDtypeStruct((b, s, aw), BF16),
                   jax.ShapeDtypeStruct((b, nheads, s // PAIR_KEYS, VT_ROWS, PAIR_KEYS), BF16),
                   jax.ShapeDtypeStruct((b, s, pw), BF16)],
        scratch_shapes=[pltpu.VMEM((POOL_HALO, pw), F32)],
        compiler_params=_params("arbitrary", "arbitrary"),
        name="in_proj",
    )(x, mod, mod, g, w_in, cos_t, sa_t, sb_t, w_pool, pool_scale, pool_gain)


def _attn_kernel(q_ref, k_ref, vt_ref, o_ref, km_ref, bias_ref, s0_ref, s1_ref, *, nb):
    blk = MOBA_BLOCK
    for j in range(nb):
        kj = k_ref[0, j * blk:(j + 1) * blk, :].astype(F32)
        km_ref[j:j + 1, :] = jnp.mean(kj, axis=0, keepdims=True)
    s_slots = (s0_ref, s1_ref)

    def tile(t, prev_accs):
        _attn_write(jnp.maximum(t - 1, 0), prev_accs, o_ref)
        return _attn_tile(t, q_ref, k_ref, vt_ref, km_ref, bias_ref, s_slots, nb)

    ntiles = nb // 2
    dummy = tuple(jnp.ones((VT_ROWS, PAIR_KEYS), F32) for _ in range(HEADS_PER_GROUP))
    _attn_write(ntiles - 1, lax.fori_loop(0, ntiles, tile, dummy), o_ref)


def _attn_write(t, accs, o_ref):
    rows = pl.ds(pl.multiple_of(t * PAIR_KEYS, PAIR_KEYS), PAIR_KEYS)
    outs = [acc[0:HEAD_DIM, :] / acc[HEAD_DIM:HEAD_DIM + 1, :] for acc in accs]
    o_ref[0, rows, :] = jnp.concatenate(outs, axis=0).T


def _attn_tile(t, q_ref, k_ref, vt_ref, km_ref, bias_ref, s_slots, nb):
    blk = MOBA_BLOCK
    tq = PAIR_KEYS
    heads = range(HEADS_PER_GROUP)
    rows = pl.ds(pl.multiple_of(t * tq, tq), tq)
    q2 = q_ref[0, rows, :]
    lane = lax.broadcasted_iota(jnp.int32, (1, LANES), 1)
    qh = [jnp.where((lane >= hh * HEAD_DIM) & (lane < (hh + 1) * HEAD_DIM), q2, jnp.zeros_like(q2))
          for hh in heads]

    def pair_keys(pi):
        return k_ref[0, pl.ds(pl.multiple_of(pi * PAIR_KEYS, PAIR_KEYS), PAIR_KEYS), :]

    def issue_scores(j, slot):
        kj = k_ref[0, pl.ds(pl.multiple_of(j * blk, blk), blk), :]
        maxes = []
        for hh in heads:
            s = lax.dot_general(kj, qh[hh], _NT, preferred_element_type=F32)
            s_slots[slot][hh] = s
            maxes.append(jnp.max(s, axis=0, keepdims=True))
        return tuple(maxes)

    def softmax_pv(hh, s, cmax, j, vt_blk, state):
        b = bias_ref[hh, j]
        m_new = jnp.maximum(state[0], cmax + b)
        p = jnp.exp2(s - (m_new - b)).astype(BF16)
        acc = jnp.exp2(state[0] - m_new) * state[1] + jnp.dot(vt_blk, p, preferred_element_type=F32)
        return m_new, acc

    km = km_ref[...]
    km_hi = km.astype(BF16)
    km_lo = (km - km_hi.astype(F32)).astype(BF16)
    kown = pair_keys(t)
    lhs = jnp.concatenate([km_hi, km_lo, kown[0:blk]], axis=0)
    r = [lax.dot_general(lhs, qh[hh], _NT, preferred_element_type=F32) for hh in heads]
    r2 = [lax.dot_general(kown[blk:], qh[hh][blk:], _NT, preferred_element_type=F32) for hh in heads]
    maxes0 = issue_scores(0, 0)

    jidx = lax.broadcasted_iota(jnp.int32, (nb, 1), 0)
    qblk = 2 * t + lax.broadcasted_iota(jnp.int32, (1, tq), 1) // blk
    past = jidx < qblk
    for hh in heads:
        gate = jnp.where(past, r[hh][0:nb] + r[hh][nb:2 * nb], NEG_INF)
        picked = jnp.zeros(gate.shape, jnp.bool_)
        for _ in range(min(MOBA_TOP_K, nb)):
            top = jnp.max(gate, axis=0, keepdims=True)
            first = jnp.min(jnp.where(gate == top, jidx, nb), axis=0, keepdims=True)
            hit = jidx == first
            picked = picked | hit
            gate = jnp.where(hit, -jnp.inf, gate)
        bias = jnp.where(picked & past, 0.0, NEG_INF).astype(F32)
        for j in range(nb):
            bias_ref[hh, j] = bias[j:j + 1, :]

    causal = (lax.broadcasted_iota(jnp.int32, (blk, blk), 0)
              <= lax.broadcasted_iota(jnp.int32, (blk, blk), 1))
    own = []
    for hh in heads:
        vt_own = vt_ref[0, hh, t]
        s_a = jnp.where(causal, r[hh][2 * nb:, 0:blk], NEG_INF)
        m_a = jnp.max(s_a, axis=0, keepdims=True)
        acc_a = jnp.dot(vt_own[:, 0:blk], jnp.exp2(s_a - m_a).astype(BF16), preferred_element_type=F32)
        s_p = r[hh][2 * nb:, blk:]
        s_o = jnp.where(causal, r2[hh], NEG_INF)
        b_p = bias_ref[hh, 2 * t][:, blk:]
        m_b = jnp.maximum(jnp.max(s_p, axis=0, keepdims=True) + b_p, jnp.max(s_o, axis=0, keepdims=True))
        p_b = jnp.concatenate([jnp.exp2(s_p - (m_b - b_p)).astype(BF16),
                               jnp.exp2(s_o - m_b).astype(BF16)], axis=0)
        acc_b = jnp.dot(vt_own, p_b, preferred_element_type=F32)
        own.append((jnp.concatenate([m_a, m_b], axis=1), jnp.concatenate([acc_a, acc_b], axis=1)))

    def trip(j, carry, slot, vt_blocks):
        state, maxes = carry
        nxt = issue_scores(jnp.minimum(j + 1, nb - 1), 1 - slot)
        return tuple(softmax_pv(hh, s_slots[slot][hh], maxes[hh], j, vt_blocks[hh], state[hh])
                     for hh in heads), nxt

    def pair_trips(u, carry):
        vt_pair = [vt_ref[0, hh, u] for hh in heads]
        carry = trip(2 * u, carry, 0, [v[:, 0:blk] for v in vt_pair])
        return trip(2 * u + 1, carry, 1, [v[:, blk:] for v in vt_pair])

    def two_pairs(v, carry):
        return pair_trips(2 * v + 1, pair_trips(2 * v, carry))

    carry = lax.fori_loop(0, t // 2, two_pairs, (tuple(own), maxes0))
    carry = lax.cond(t % 2 == 1, lambda: pair_trips(t - 1, carry), lambda: carry)
    return tuple(acc for _, acc in carry[0])


def _attn_call(q, k, vt):
    b, s, aw = q.shape
    ngroups = aw // LANES
    nb = s // MOBA_BLOCK
    tq = PAIR_KEYS
    return pl.pallas_call(
        functools.partial(_attn_kernel, nb=nb),
        grid=(b, ngroups),
        in_specs=[pl.BlockSpec((1, s, LANES), lambda bi, gi: (bi, 0, gi)),
                  pl.BlockSpec((1, s, LANES), lambda bi, gi: (bi, 0, gi)),
                  pl.BlockSpec((1, HEADS_PER_GROUP, nb // 2, VT_ROWS, PAIR_KEYS),
                               lambda bi, gi: (bi, gi, 0, 0, 0))],
        out_specs=pl.BlockSpec((1, s, LANES), lambda bi, gi: (bi, 0, gi)),
        out_shape=jax.ShapeDtypeStruct((b, s, aw), F32),
        scratch_shapes=[pltpu.VMEM((nb, LANES), F32),
                        pltpu.VMEM((HEADS_PER_GROUP, nb, 1, tq), F32),
                        pltpu.VMEM((HEADS_PER_GROUP, MOBA_BLOCK, tq), F32),
                        pltpu.VMEM((HEADS_PER_GROUP, MOBA_BLOCK, tq), F32)],
        compiler_params=_params("arbitrary", "arbitrary"),
        name="moba_attn",
    )(q, k, vt)


def _ffn_kernel(ao_ref, pn_ref, xin_ref, ga_ref, wo_ref, gpm_ref, gt1_ref,
                sc_ref, sh_ref, g_ref, wu_ref, cw_ref, cb_ref, wd_ref, gp_ref, gt_ref,
                o_ref, tail_ref, act_ref, *, ts, tf):
    si = pl.program_id(1)
    aw = ao_ref.shape[2]
    an = (_rms(ao_ref[0]) * ga_ref[...]).astype(BF16)
    y = (jnp.dot(an, wo_ref[0:aw, :], preferred_element_type=F32)
         + jnp.dot(pn_ref[0], wo_ref[aw:, :], preferred_element_type=F32))
    x = xin_ref[0] + _rms(y) * (gt1_ref[0] * gpm_ref[...])

    hb = (_rms(x) * (g_ref[...] * (1.0 + sc_ref[0])) + sh_ref[0]).astype(BF16)
    d_ff = wd_ref.shape[0]
    first = si == 0

    def conv(lo):
        u = jnp.dot(hb, wu_ref[:, lo:lo + tf], preferred_element_type=F32)
        tail = jnp.where(first, 0.0, tail_ref[:, lo:lo + tf])
        tail_ref[:, lo:lo + tf] = u[ts - CONV_HALO:, :]
        ucat = jnp.concatenate([tail, u], axis=0)
        uc = cb_ref[:, lo:lo + tf] + cw_ref[CONV_WIDTH - 1:CONV_WIDTH, lo:lo + tf] * u
        for j in range(CONV_WIDTH - 1):
            back = CONV_WIDTH - 1 - j
            uc = uc + cw_ref[j:j + 1, lo:lo + tf] * ucat[CONV_HALO - back:CONV_HALO - back + ts, :]
        return uc

    for fi in range(d_ff // tf):
        a = conv(fi * tf)
        g = conv(d_ff + fi * tf)
        act_ref[:, fi * tf:(fi + 1) * tf] = (a * jax.nn.sigmoid(a) * g).astype(BF16)
    y = jnp.dot(act_ref[...], wd_ref[...], preferred_element_type=F32)
    o_ref[0] = x + _rms(y) * (gt_ref[0] * gp_ref[...])


def _ffn_call(l, ao, pn, x, mod, attn_gain, w_out, g_post_mix,
              g_pre, w_up, conv_w, conv_b, w_down, g_post, ts=1024, tf=256):
    b, s, d = x.shape
    f = w_down.shape[1]
    aw = ao.shape[2]
    pw = pn.shape[2]
    tok = lambda bi, si: (bi, si, 0)
    resident = dict(pipeline_mode=pl.Buffered(1))
    return pl.pallas_call(
        functools.partial(_ffn_kernel, ts=ts, tf=tf),
        grid=(b, s // ts),
        in_specs=[pl.BlockSpec((1, ts, aw), tok),
                  pl.BlockSpec((1, ts, pw), tok),
                  pl.BlockSpec((1, ts, d), tok),
                  _layer_spec(l, attn_gain),
                  _layer_spec(l, w_out, **resident),
                  _layer_spec(l, g_post_mix),
                  _mod_spec(l, MOD_GT1, d),
                  _mod_spec(l, MOD_SC2, d),
                  _mod_spec(l, MOD_SH2, d),
                  _layer_spec(l, g_pre),
                  _layer_spec(l, w_up, **resident),
                  _layer_spec(l, conv_w),
                  _layer_spec(l, conv_b),
                  _layer_spec(l, w_down, **resident),
                  _layer_spec(l, g_post),
                  _mod_spec(l, MOD_GT2, d)],
        out_specs=pl.BlockSpec((1, ts, d), tok),
        out_shape=jax.ShapeDtypeStruct((b, s, d), F32),
        scratch_shapes=[pltpu.VMEM((CONV_HALO, 2 * f), F32),
                        pltpu.VMEM((ts, f), BF16)],
        compiler_params=_params("arbitrary", "arbitrary"),
        name="conv_ffn",
    )(ao, pn, x, attn_gain, w_out, g_post_mix, mod,
      mod, mod, g_pre, w_up, conv_w, conv_b, w_down, g_post, mod)


def _rotary_tables(s):
    half = ROT_DIM // 2
    pos = jnp.arange(s, dtype=F32)
    inv_freq = jnp.power(jnp.float32(ROPE_THETA), -jnp.arange(0, ROT_DIM, 2, dtype=F32) / ROT_DIM)
    dim = np.arange(LANES) % HEAD_DIM
    freq = jnp.where(dim < ROT_DIM, inv_freq[dim % half], 0.0)
    ang = pos[:, None] * freq[None, :]
    cos = jnp.cos(ang)
    sin = jnp.sin(ang)
    sa = jnp.where(dim < half, -sin, 0.0)
    sb = jnp.where((dim >= half) & (dim < ROT_DIM), sin, 0.0)
    return cos, sa, sb


def kernel(x, c, w_ada, b_ada, g_pre_mix, w_in, w_pool, pool_scale, attn_out_gain, pool_out_gain,
           w_out, g_post_mix, g_pre_ffn, w_up, conv_w, conv_b, w_down, g_post_ffn):
    b, s, d = x.shape
    depth = w_ada.shape[0]
    assert s % PAIR_KEYS == 0 and d % LANES == 0
    mod = _mod_call(c, w_ada, b_ada).reshape(depth, b, 6, 1, d)
    cos_t, sa_t, sb_t = _rotary_tables(s)
    w_in, w_pool, w_out, w_up, w_down = (w.astype(BF16) for w in (w_in, w_pool, w_out, w_up, w_down))
    (g_pre_mix, pool_scale, pool_out_gain, attn_out_gain, g_post_mix, g_pre_ffn, conv_b,
     g_post_ffn) = (v[:, None, :] for v in (g_pre_mix, pool_scale, pool_out_gain, attn_out_gain,
                                            g_post_mix, g_pre_ffn, conv_b, g_post_ffn))
    for l in range(depth):
        q, k, vt, pn = _in_call(l, x, mod, g_pre_mix, w_in, cos_t, sa_t, sb_t, w_pool,
                                pool_scale, pool_out_gain)
        ao = _attn_call(q, k, vt)
        x = _ffn_call(l, ao, pn, x, mod, attn_out_gain, w_out, g_post_mix,
                      g_pre_ffn, w_up, conv_w, conv_b, w_down, g_post_ffn)
    return x
```

```python
import functools

import jax
import jax.numpy as jnp
import numpy as np
from jax import lax
from jax.experimental import pallas as pl
from jax.experimental.pallas import tpu as pltpu

F32 = jnp.float32
BF16 = jnp.bfloat16

HEAD_DIM = 64
HEADS_PER_GROUP = 2
LANES = 128
MXU_COLS = 256
MOBA_BLOCK = 256
MOBA_TOP_K = 3
POOL_WINDOWS = (2, 4, 8, 16)
POOL_HALO = 16
ROT_DIM = HEAD_DIM // 4
ROPE_THETA = 500000.0
CONV_WIDTH = 3
CONV_HALO = 8
NORM_PARTS = 4
NORM_EPS = 1e-6
NEG_INF = -1e30
LOG2_E = 1.4426950408889634
VT_ROWS = HEAD_DIM + 16
PAIR_KEYS = 2 * MOBA_BLOCK
VMEM_LIMIT = 56 * 1024 * 1024
MOD_SH1, MOD_SC1, MOD_GT1, MOD_SH2, MOD_SC2, MOD_GT2 = range(6)

_NT = (((1,), (1,)), ((), ()))


def _rms(x):
    return x * lax.rsqrt(jnp.mean(x * x, axis=-1, keepdims=True) + NORM_EPS)


def _params(*sem):
    return pltpu.CompilerParams(dimension_semantics=sem, vmem_limit_bytes=VMEM_LIMIT)


def _mod_kernel(c_ref, w_ref, b_ref, o_ref):
    c = c_ref[...]
    c_act = c * jax.nn.sigmoid(c)
    o_ref[0] = jnp.dot(c_act.astype(BF16), w_ref[0].astype(BF16),
                       preferred_element_type=F32) + b_ref[0]


def _mod_call(c, w_ada, b_ada, tn=1536):
    nl, d, n6 = w_ada.shape
    b = c.shape[0]
    return pl.pallas_call(
        _mod_kernel,
        grid=(nl, n6 // tn),
        in_specs=[pl.BlockSpec((b, d), lambda l, j: (0, 0)),
                  pl.BlockSpec((1, d, tn), lambda l, j: (l, 0, j)),
                  pl.BlockSpec((1, 1, tn), lambda l, j: (l, 0, j))],
        out_specs=pl.BlockSpec((1, b, tn), lambda l, j: (l, 0, j)),
        out_shape=jax.ShapeDtypeStruct((nl, b, n6), F32),
        compiler_params=_params("arbitrary", "arbitrary"),
        name="adaln_mod",
    )(c, w_ada, b_ada.reshape(nl, 1, n6))


def _in_kernel(x_ref, sc_ref, sh_ref, g_ref, w_ref, cos_ref, sa_ref, sb_ref, wp_ref, ps_ref,
               pg_ref, q_ref, k_ref, vt_ref, pn_ref, halo_ref, *, ts, aw):
    si = pl.program_id(1)
    h = _rms(x_ref[0]) * (g_ref[...] * (1.0 + sc_ref[0])) + sh_ref[0]
    hb = h.astype(BF16)
    def proj(lo, hi):
        return jnp.concatenate([jnp.dot(hb, w_ref[:, c:c + MXU_COLS], preferred_element_type=F32)
                                for c in range(lo, hi, MXU_COLS)], axis=1)

    pz = proj(3 * aw, w_ref.shape[1])
    z = proj(0, 3 * aw)

    cos = cos_ref[...]
    sa = sa_ref[...]
    sb = sb_ref[...]

    def rot(t):
        return t * cos + pltpu.roll(t, LANES - ROT_DIM // 2, 1) * sa + pltpu.roll(t, ROT_DIM // 2, 1) * sb

    scale = HEAD_DIM ** -0.5 * LOG2_E
    ngroups = aw // LANES
    ones = jnp.ones((VT_ROWS - HEAD_DIM, PAIR_KEYS), BF16)
    for gi in range(ngroups):
        lo = gi * LANES
        q_ref[0, :, lo:lo + LANES] = (rot(z[:, lo:lo + LANES]) * scale).astype(BF16)
        k_ref[0, :, lo:lo + LANES] = rot(z[:, aw + lo:aw + lo + LANES]).astype(BF16)
        vt = z[:, 2 * aw + lo:2 * aw + lo + LANES].T.astype(BF16)
        for hh in range(HEADS_PER_GROUP):
            head = gi * HEADS_PER_GROUP + hh
            for jp in range(ts // PAIR_KEYS):
                vt_ref[0, head, jp, 0:HEAD_DIM, :] = vt[hh * HEAD_DIM:(hh + 1) * HEAD_DIM,
                                                        jp * PAIR_KEYS:(jp + 1) * PAIR_KEYS]
                vt_ref[0, head, jp, HEAD_DIM:VT_ROWS, :] = ones

    prev = jnp.where(si == 0, 0.0, halo_ref[...])
    halo_ref[...] = pz[ts - POOL_HALO:, :]
    ext = jnp.concatenate([prev, pz], axis=0)
    tpos = si * ts + lax.broadcasted_iota(jnp.int32, (ts, 1), 0)
    outs = []
    for g, w in enumerate(POOL_WINDOWS):
        lo = g * LANES
        s = ext[:, lo:lo + LANES]
        span = 1
        while span < w:
            s = s[span:] + s[:-span]
            span *= 2
        s = s[POOL_HALO - (w - 1):POOL_HALO - (w - 1) + ts]
        cnt = jnp.minimum(tpos + 1, w).astype(F32)
        pooled = s / cnt - pz[:, lo:lo + LANES]
        outs.append(jnp.dot(pooled.astype(BF16), wp_ref[g], preferred_element_type=F32))
    po = jnp.concatenate(outs, axis=1) * ps_ref[...]
    pn_ref[0] = (_rms(po) * pg_ref[...]).astype(BF16)


def _mod_spec(l, chunk, d):
    return pl.BlockSpec((None, None, 1, 1, d), lambda bi, si: (l, bi, chunk, 0, 0))


def _layer_spec(l, arr, **kw):
    return pl.BlockSpec((None,) + arr.shape[1:], lambda bi, si: (l,) + (0,) * (arr.ndim - 1), **kw)


def _in_call(l, x, mod, g, w_in, cos_t, sa_t, sb_t, w_pool, pool_scale, pool_gain, ts=1024):
    b, s, d = x.shape
    n_in = w_in.shape[2]
    pw = w_pool.shape[1] * w_pool.shape[2]
    aw = (n_in - pw) // 3
    nheads = aw // HEAD_DIM
    tok = lambda bi, si: (bi, si, 0)
    return pl.pallas_call(
        functools.partial(_in_kernel, ts=ts, aw=aw),
        grid=(b, s // ts),
        in_specs=[pl.BlockSpec((1, ts, d), tok),
                  _mod_spec(l, MOD_SC1, d),
                  _mod_spec(l, MOD_SH1, d),
                  _layer_spec(l, g),
                  _layer_spec(l, w_in),
                  pl.BlockSpec((ts, LANES), lambda bi, si: (si, 0)),
                  pl.BlockSpec((ts, LANES), lambda bi, si: (si, 0)),
                  pl.BlockSpec((ts, LANES), lambda bi, si: (si, 0)),
                  _layer_spec(l, w_pool),
                  _layer_spec(l, pool_scale),
                  _layer_spec(l, pool_gain)],
        out_specs=[pl.BlockSpec((1, ts, aw), tok),
                   pl.BlockSpec((1, ts, aw), tok),
                   pl.BlockSpec((1, nheads, ts // PAIR_KEYS, VT_ROWS, PAIR_KEYS),
                                lambda bi, si: (bi, 0, si, 0, 0)),
                   pl.BlockSpec((1, ts, pw), tok)],
        out_shape=[jax.ShapeDtypeStruct((b, s, aw), BF16),
                   jax.ShapeDtypeStruct((b, s, aw), BF16),
                   jax.ShapeDtypeStruct((b, nheads, s // PAIR_KEYS, VT_ROWS, PAIR_KEYS), BF16),
                   jax.ShapeDtypeStruct((b, s, pw), BF16)],
        scratch_shapes=[pltpu.VMEM((POOL_HALO, pw), F32)],
        compiler_params=_params("arbitrary", "arbitrary"),
        name="in_proj",
    )(x, mod, mod, g, w_in, cos_t, sa_t, sb_t, w_pool, pool_scale, pool_gain)


def _attn_kernel(q_ref, k_ref, vt_ref, o_ref, km_ref, bias_ref, s0_ref, s1_ref, *, nb):
    blk = MOBA_BLOCK
    for j in range(nb):
        kj = k_ref[0, j * blk:(j + 1) * blk, :].astype(F32)
        km_ref[j:j + 1, :] = jnp.mean(kj, axis=0, keepdims=True)
    s_slots = (s0_ref, s1_ref)

    def tile(t, prev_accs):
        _attn_write(jnp.maximum(t - 1, 0), prev_accs, o_ref)
        return _attn_tile(t, q_ref, k_ref, vt_ref, km_ref, bias_ref, s_slots, nb)

    ntiles = nb // 2
    dummy = tuple(jnp.ones((VT_ROWS, PAIR_KEYS), F32) for _ in range(HEADS_PER_GROUP))
    _attn_write(ntiles - 1, lax.fori_loop(0, ntiles, tile, dummy), o_ref)


def _attn_write(t, accs, o_ref):
    rows = pl.ds(pl.multiple_of(t * PAIR_KEYS, PAIR_KEYS), PAIR_KEYS)
    outs = [acc[0:HEAD_DIM, :] / acc[HEAD_DIM:HEAD_DIM + 1, :] for acc in accs]
    o_ref[0, rows, :] = jnp.concatenate(outs, axis=0).T


def _attn_tile(t, q_ref, k_ref, vt_ref, km_ref, bias_ref, s_slots, nb):
    blk = MOBA_BLOCK
    tq = PAIR_KEYS
    heads = range(HEADS_PER_GROUP)
    rows = pl.ds(pl.multiple_of(t * tq, tq), tq)
    q2 = q_ref[0, rows, :]
    lane = lax.broadcasted_iota(jnp.int32, (1, LANES), 1)
    qh = [jnp.where((lane >= hh * HEAD_DIM) & (lane < (hh + 1) * HEAD_DIM), q2, jnp.zeros_like(q2))
          for hh in heads]

    def pair_keys(pi):
        return k_ref[0, pl.ds(pl.multiple_of(pi * PAIR_KEYS, PAIR_KEYS), PAIR_KEYS), :]

    def issue_scores(j, slot):
        kj = k_ref[0, pl.ds(pl.multiple_of(j * blk, blk), blk), :]
        maxes = []
        for hh in heads:
            s = lax.dot_general(kj, qh[hh], _NT, preferred_element_type=F32)
            s_slots[slot][hh] = s
            maxes.append(jnp.max(s, axis=0, keepdims=True))
        return tuple(maxes)

    def softmax_pv(hh, s, cmax, j, vt_blk, state):
        b = bias_ref[hh, j]
        m_new = jnp.maximum(state[0], cmax + b)
        p = jnp.exp2(s - (m_new - b)).astype(BF16)
        acc = jnp.exp2(state[0] - m_new) * state[1] + jnp.dot(vt_blk, p, preferred_element_type=F32)
        return m_new, acc

    km = km_ref[...]
    km_hi = km.astype(BF16)
    km_lo = (km - km_hi.astype(F32)).astype(BF16)
    kown = pair_keys(t)
    lhs = jnp.concatenate([km_hi, km_lo, kown[0:blk]], axis=0)
    r = [lax.dot_general(lhs, qh[hh], _NT, preferred_element_type=F32) for hh in heads]
    r2 = [lax.dot_general(kown[blk:], qh[hh][blk:], _NT, preferred_element_type=F32) for hh in heads]
    maxes0 = issue_scores(0, 0)

    jidx = lax.broadcasted_iota(jnp.int32, (nb, 1), 0)
    qblk = 2 * t + lax.broadcasted_iota(jnp.int32, (1, tq), 1) // blk
    past = jidx < qblk
    for hh in heads:
        gate = jnp.where(past, r[hh][0:nb] + r[hh][nb:2 * nb], NEG_INF)
        picked = jnp.zeros(gate.shape, jnp.bool_)
        for _ in range(min(MOBA_TOP_K, nb)):
            top = jnp.max(gate, axis=0, keepdims=True)
            first = jnp.min(jnp.where(gate == top, jidx, nb), axis=0, keepdims=True)
            hit = jidx == first
            picked = picked | hit
            gate = jnp.where(hit, -jnp.inf, gate)
        bias = jnp.where(picked & past, 0.0, NEG_INF).astype(F32)
        for j in range(nb):
            bias_ref[hh, j] = bias[j:j + 1, :]

    causal = (lax.broadcasted_iota(jnp.int32, (blk, blk), 0)
              <= lax.broadcasted_iota(jnp.int32, (blk, blk), 1))
    own = []
    for hh in heads:
        vt_own = vt_ref[0, hh, t]
        s_a = jnp.where(causal, r[hh][2 * nb:, 0:blk], NEG_INF)
        m_a = jnp.max(s_a, axis=0, keepdims=True)
        acc_a = jnp.dot(vt_own[:, 0:blk], jnp.exp2(s_a - m_a).astype(BF16), preferred_element_type=F32)
        s_p = r[hh][2 * nb:, blk:]
        s_o = jnp.where(causal, r2[hh], NEG_INF)
        b_p = bias_ref[hh, 2 * t][:, blk:]
        m_b = jnp.maximum(jnp.max(s_p, axis=0, keepdims=True) + b_p, jnp.max(s_o, axis=0, keepdims=True))
        p_b = jnp.concatenate([jnp.exp2(s_p - (m_b - b_p)).astype(BF16),
                               jnp.exp2(s_o - m_b).astype(BF16)], axis=0)
        acc_b = jnp.dot(vt_own, p_b, preferred_element_type=F32)
        own.append((jnp.concatenate([m_a, m_b], axis=1), jnp.concatenate([acc_a, acc_b], axis=1)))

    def trip(j, carry, slot, vt_blocks):
        state, maxes = carry
        nxt = issue_scores(jnp.minimum(j + 1, nb - 1), 1 - slot)
        return tuple(softmax_pv(hh, s_slots[slot][hh], maxes[hh], j, vt_blocks[hh], state[hh])
                     for hh in heads), nxt

    def pair_trips(u, carry):
        vt_pair = [vt_ref[0, hh, u] for hh in heads]
        carry = trip(2 * u, carry, 0, [v[:, 0:blk] for v in vt_pair])
        return trip(2 * u + 1, carry, 1, [v[:, blk:] for v in vt_pair])

    def two_pairs(v, carry):
        return pair_trips(2 * v + 1, pair_trips(2 * v, carry))

    carry = lax.fori_loop(0, t // 2, two_pairs, (tuple(own), maxes0))
    carry = lax.cond(t % 2 == 1, lambda: pair_trips(t - 1, carry), lambda: carry)
    return tuple(acc for _, acc in carry[0])


def _attn_call(q, k, vt):
    b, s, aw = q.shape
    ngroups = aw // LANES
    nb = s // MOBA_BLOCK
    tq = PAIR_KEYS
    return pl.pallas_call(
        functools.partial(_attn_kernel, nb=nb),
        grid=(b, ngroups),
        in_specs=[pl.BlockSpec((1, s, LANES), lambda bi, gi: (bi, 0, gi)),
                  pl.BlockSpec((1, s, LANES), lambda bi, gi: (bi, 0, gi)),
                  pl.BlockSpec((1, HEADS_PER_GROUP, nb // 2, VT_ROWS, PAIR_KEYS),
                               lambda bi, gi: (bi, gi, 0, 0, 0))],
        out_specs=pl.BlockSpec((1, s, LANES), lambda bi, gi: (bi, 0, gi)),
        out_shape=jax.ShapeDtypeStruct((b, s, aw), F32),
        scratch_shapes=[pltpu.VMEM((nb, LANES), F32),
                        pltpu.VMEM((HEADS_PER_GROUP, nb, 1, tq), F32),
                        pltpu.VMEM((HEADS_PER_GROUP, MOBA_BLOCK, tq), F32),
                        pltpu.VMEM((HEADS_PER_GROUP, MOBA_BLOCK, tq), F32)],
        compiler_params=_params("arbitrary", "arbitrary"),
        name="moba_attn",
    )(q, k, vt)


def _ffn_kernel(ao_ref, pn_ref, xin_ref, ga_ref, wo_ref, gpm_ref, gt1_ref,
                sc_ref, sh_ref, g_ref, wu_ref, cw_ref, cb_ref, wd_ref, gp_ref, gt_ref,
                o_ref, tail_ref, act_ref, *, ts, tf):
    si = pl.program_id(1)
    aw = ao_ref.shape[2]
    parts = [slice(r * (ts // NORM_PARTS), (r + 1) * (ts // NORM_PARTS)) for r in range(NORM_PARTS)]
    xs, hbs = [], []
    for rows in parts:
        an = (_rms(ao_ref[0, rows, :]) * ga_ref[...]).astype(BF16)
        y = (jnp.dot(an, wo_ref[0:aw, :], preferred_element_type=F32)
             + jnp.dot(pn_ref[0, rows, :], wo_ref[aw:, :], preferred_element_type=F32))
        x = xin_ref[0, rows, :] + _rms(y) * (gt1_ref[0] * gpm_ref[...])
        xs.append(x)
        hbs.append((_rms(x) * (g_ref[...] * (1.0 + sc_ref[0])) + sh_ref[0]).astype(BF16))

    hb = jnp.concatenate(hbs, axis=0)
    d_ff = wd_ref.shape[0]
    first = si == 0

    def conv(lo):
        u = jnp.dot(hb, wu_ref[:, lo:lo + tf], preferred_element_type=F32)
        tail = jnp.where(first, 0.0, tail_ref[:, lo:lo + tf])
        tail_ref[:, lo:lo + tf] = u[ts - CONV_HALO:, :]
        ucat = jnp.concatenate([tail, u], axis=0)
        uc = cb_ref[:, lo:lo + tf] + cw_ref[CONV_WIDTH - 1:CONV_WIDTH, lo:lo + tf] * u
        for j in range(CONV_WIDTH - 1):
            back = CONV_WIDTH - 1 - j
            uc = uc + cw_ref[j:j + 1, lo:lo + tf] * ucat[CONV_HALO - back:CONV_HALO - back + ts, :]
        return uc

    for fi in range(d_ff // tf):
        a = conv(fi * tf)
        g = conv(d_ff + fi * tf)
        act_ref[:, fi * tf:(fi + 1) * tf] = (a * jax.nn.sigmoid(a) * g).astype(BF16)
    for rows, x in zip(parts, xs):
        y = jnp.dot(act_ref[rows, :], wd_ref[...], preferred_element_type=F32)
        o_ref[0, rows, :] = x + _rms(y) * (gt_ref[0] * gp_ref[...])


def _ffn_call(l, ao, pn, x, mod, attn_gain, w_out, g_post_mix,
              g_pre, w_up, conv_w, conv_b, w_down, g_post, ts=1024, tf=256):
    b, s, d = x.shape
    f = w_down.shape[1]
    aw = ao.shape[2]
    pw = pn.shape[2]
    tok = lambda bi, si: (bi, si, 0)
    resident = dict(pipeline_mode=pl.Buffered(1))
    return pl.pallas_call(
        functools.partial(_ffn_kernel, ts=ts, tf=tf),
        grid=(b, s // ts),
        in_specs=[pl.BlockSpec((1, ts, aw), tok),
                  pl.BlockSpec((1, ts, pw), tok),
                  pl.BlockSpec((1, ts, d), tok),
                  _layer_spec(l, attn_gain),
                  _layer_spec(l, w_out, **resident),
                  _layer_spec(l, g_post_mix),
                  _mod_spec(l, MOD_GT1, d),
                  _mod_spec(l, MOD_SC2, d),
                  _mod_spec(l, MOD_SH2, d),
                  _layer_spec(l, g_pre),
                  _layer_spec(l, w_up, **resident),
                  _layer_spec(l, conv_w),
                  _layer_spec(l, conv_b),
                  _layer_spec(l, w_down, **resident),
                  _layer_spec(l, g_post),
                  _mod_spec(l, MOD_GT2, d)],
        out_specs=pl.BlockSpec((1, ts, d), tok),
        out_shape=jax.ShapeDtypeStruct((b, s, d), F32),
        scratch_shapes=[pltpu.VMEM((CONV_HALO, 2 * f), F32),
                        pltpu.VMEM((ts, f), BF16)],
        compiler_params=_params("arbitrary", "arbitrary"),
        name="conv_ffn",
    )(ao, pn, x, attn_gain, w_out, g_post_mix, mod,
      mod, mod, g_pre, w_up, conv_w, conv_b, w_down, g_post, mod)


def _rotary_tables(s):
    half = ROT_DIM // 2
    pos = jnp.arange(s, dtype=F32)
    inv_freq = jnp.power(jnp.float32(ROPE_THETA), -jnp.arange(0, ROT_DIM, 2, dtype=F32) / ROT_DIM)
    dim = np.arange(LANES) % HEAD_DIM
    freq = jnp.where(dim < ROT_DIM, inv_freq[dim % half], 0.0)
    ang = pos[:, None] * freq[None, :]
    cos = jnp.cos(ang)
    sin = jnp.sin(ang)
    sa = jnp.where(dim < half, -sin, 0.0)
    sb = jnp.where((dim >= half) & (dim < ROT_DIM), sin, 0.0)
    return cos, sa, sb


def kernel(x, c, w_ada, b_ada, g_pre_mix, w_in, w_pool, pool_scale, attn_out_gain, pool_out_gain,
           w_out, g_post_mix, g_pre_ffn, w_up, conv_w, conv_b, w_down, g_post_ffn):
    b, s, d = x.shape
    depth = w_ada.shape[0]
    assert s % PAIR_KEYS == 0 and d % LANES == 0
    mod = _mod_call(c, w_ada, b_ada).reshape(depth, b, 6, 1, d)
    cos_t, sa_t, sb_t = _rotary_tables(s)
    w_in, w_pool, w_out, w_up, w_down = (w.astype(BF16) for w in (w_in, w_pool, w_out, w_up, w_down))
    (g_pre_mix, pool_scale, pool_out_gain, attn_out_gain, g_post_mix, g_pre_ffn, conv_b,
     g_post_ffn) = (v[:, None, :] for v in (g_pre_mix, pool_scale, pool_out_gain, attn_out_gain,
                                            g_post_mix, g_pre_ffn, conv_b, g_post_ffn))
    for l in range(depth):
        q, k, vt, pn = _in_call(l, x, mod, g_pre_mix, w_in, cos_t, sa_t, sb_t, w_pool,
                                pool_scale, pool_out_gain)
        ao = _attn_call(q, k, vt)
        x = _ffn_call(l, ao, pn, x, mod, attn_out_gain, w_out, g_post_mix,
                      g_pre_ffn, w_up, conv_w, conv_b, w_down, g_post_ffn)
    return x
```

```python
import functools

import jax
import jax.numpy as jnp
import numpy as np
from jax import lax
from jax.experimental import pallas as pl
from jax.experimental.pallas import tpu as pltpu

F32 = jnp.float32
BF16 = jnp.bfloat16

HEAD_DIM = 64
HEADS_PER_GROUP = 2
LANES = 128
MXU_COLS = 256
MOBA_BLOCK = 256
MOBA_TOP_K = 3
POOL_WINDOWS = (2, 4, 8, 16)
POOL_HALO = 16
ROT_DIM = HEAD_DIM // 4
ROPE_THETA = 500000.0
CONV_WIDTH = 3
CONV_HALO = 8
TOKEN_TILE = 1024
NORM_PARTS = 4
NORM_EPS = 1e-6
NEG_INF = -1e30
LOG2_E = 1.4426950408889634
VT_ROWS = HEAD_DIM + 16
PAIR_KEYS = 2 * MOBA_BLOCK
VMEM_LIMIT = 56 * 1024 * 1024
MOD_SH1, MOD_SC1, MOD_GT1, MOD_SH2, MOD_SC2, MOD_GT2 = range(6)

_NT = (((1,), (1,)), ((), ()))


def _rms(x):
    return x * lax.rsqrt(jnp.mean(x * x, axis=-1, keepdims=True) + NORM_EPS)


def _params(*sem):
    return pltpu.CompilerParams(dimension_semantics=sem, vmem_limit_bytes=VMEM_LIMIT)


def _mod_kernel(c_ref, w_ref, b_ref, o_ref):
    c = c_ref[...]
    c_act = c * jax.nn.sigmoid(c)
    o_ref[0] = jnp.dot(c_act.astype(BF16), w_ref[0].astype(BF16),
                       preferred_element_type=F32) + b_ref[0]


def _mod_call(c, w_ada, b_ada, tn=1536):
    nl, d, n6 = w_ada.shape
    b = c.shape[0]
    return pl.pallas_call(
        _mod_kernel,
        grid=(nl, n6 // tn),
        in_specs=[pl.BlockSpec((b, d), lambda l, j: (0, 0)),
                  pl.BlockSpec((1, d, tn), lambda l, j: (l, 0, j)),
                  pl.BlockSpec((1, 1, tn), lambda l, j: (l, 0, j))],
        out_specs=pl.BlockSpec((1, b, tn), lambda l, j: (l, 0, j)),
        out_shape=jax.ShapeDtypeStruct((nl, b, n6), F32),
        compiler_params=_params("arbitrary", "arbitrary"),
        name="adaln_mod",
    )(c, w_ada, b_ada.reshape(nl, 1, n6))


def _in_kernel(x_ref, sc_ref, sh_ref, g_ref, w_ref, cos_ref, sa_ref, sb_ref, wp_ref, ps_ref,
               pg_ref, q_ref, k_ref, vt_ref, pn_ref, halo_ref, *, ts, aw):
    si = pl.program_id(1)
    h = _rms(x_ref[0]) * (g_ref[...] * (1.0 + sc_ref[0])) + sh_ref[0]
    hb = h.astype(BF16)
    def proj(lo, hi):
        return jnp.concatenate([jnp.dot(hb, w_ref[:, c:c + MXU_COLS], preferred_element_type=F32)
                                for c in range(lo, hi, MXU_COLS)], axis=1)

    pz = proj(3 * aw, w_ref.shape[1])
    z = proj(0, 3 * aw)

    cos = cos_ref[...]
    sa = sa_ref[...]
    sb = sb_ref[...]

    def rot(t):
        return t * cos + pltpu.roll(t, LANES - ROT_DIM // 2, 1) * sa + pltpu.roll(t, ROT_DIM // 2, 1) * sb

    scale = HEAD_DIM ** -0.5 * LOG2_E
    ngroups = aw // LANES
    ones = jnp.ones((VT_ROWS - HEAD_DIM, PAIR_KEYS), BF16)
    for gi in range(ngroups):
        lo = gi * LANES
        q_ref[0, :, lo:lo + LANES] = (rot(z[:, lo:lo + LANES]) * scale).astype(BF16)
        k_ref[0, :, lo:lo + LANES] = rot(z[:, aw + lo:aw + lo + LANES]).astype(BF16)
        vt = z[:, 2 * aw + lo:2 * aw + lo + LANES].T.astype(BF16)
        for hh in range(HEADS_PER_GROUP):
            head = gi * HEADS_PER_GROUP + hh
            for jp in range(ts // PAIR_KEYS):
                vt_ref[0, head, jp, 0:HEAD_DIM, :] = vt[hh * HEAD_DIM:(hh + 1) * HEAD_DIM,
                                                        jp * PAIR_KEYS:(jp + 1) * PAIR_KEYS]
                vt_ref[0, head, jp, HEAD_DIM:VT_ROWS, :] = ones

    prev = jnp.where(si == 0, 0.0, halo_ref[...])
    halo_ref[...] = pz[ts - POOL_HALO:, :]
    ext = jnp.concatenate([prev, pz], axis=0)
    tpos = si * ts + lax.broadcasted_iota(jnp.int32, (ts, 1), 0)
    outs = []
    for g, w in enumerate(POOL_WINDOWS):
        lo = g * LANES
        s = ext[:, lo:lo + LANES]
        span = 1
        while span < w:
            s = s[span:] + s[:-span]
            span *= 2
        s = s[POOL_HALO - (w - 1):POOL_HALO - (w - 1) + ts]
        cnt = jnp.minimum(tpos + 1, w).astype(F32)
        pooled = s / cnt - pz[:, lo:lo + LANES]
        outs.append(jnp.dot(pooled.astype(BF16), wp_ref[g], preferred_element_type=F32))
    po = jnp.concatenate(outs, axis=1) * ps_ref[...]
    pn_ref[0] = (_rms(po) * pg_ref[...]).astype(BF16)


def _mod_spec(l, chunk, d):
    return pl.BlockSpec((None, None, 1, 1, d), lambda bi, si: (l, bi, chunk, 0, 0))


def _layer_spec(l, arr, **kw):
    return pl.BlockSpec((None,) + arr.shape[1:], lambda bi, si: (l,) + (0,) * (arr.ndim - 1), **kw)


def _in_call(l, x, mod, g, w_in, cos_t, sa_t, sb_t, w_pool, pool_scale, pool_gain, ts=TOKEN_TILE):
    b, s, d = x.shape
    n_in = w_in.shape[2]
    pw = w_pool.shape[1] * w_pool.shape[2]
    aw = (n_in - pw) // 3
    nheads = aw // HEAD_DIM
    tok = lambda bi, si: (bi, si, 0)
    return pl.pallas_call(
        functools.partial(_in_kernel, ts=ts, aw=aw),
        grid=(b, s // ts),
        in_specs=[pl.BlockSpec((1, ts, d), tok),
                  _mod_spec(l, MOD_SC1, d),
                  _mod_spec(l, MOD_SH1, d),
                  _layer_spec(l, g),
                  _layer_spec(l, w_in),
                  pl.BlockSpec((ts, LANES), lambda bi, si: (si, 0)),
                  pl.BlockSpec((ts, LANES), lambda bi, si: (si, 0)),
                  pl.BlockSpec((ts, LANES), lambda bi, si: (si, 0)),
                  _layer_spec(l, w_pool),
                  _layer_spec(l, pool_scale),
                  _layer_spec(l, pool_gain)],
        out_specs=[pl.BlockSpec((1, ts, aw), tok),
                   pl.BlockSpec((1, ts, aw), tok),
                   pl.BlockSpec((1, nheads, ts // PAIR_KEYS, VT_ROWS, PAIR_KEYS),
                                lambda bi, si: (bi, 0, si, 0, 0)),
                   pl.BlockSpec((1, ts, pw), tok)],
        out_shape=[jax.ShapeDtypeStruct((b, s, aw), BF16),
                   jax.ShapeDtypeStruct((b, s, aw), BF16),
                   jax.ShapeDtypeStruct((b, nheads, s // PAIR_KEYS, VT_ROWS, PAIR_KEYS), BF16),
                   jax.ShapeDtypeStruct((b, s, pw), BF16)],
        scratch_shapes=[pltpu.VMEM((POOL_HALO, pw), F32)],
        compiler_params=_params("arbitrary", "arbitrary"),
        name="in_proj",
    )(x, mod, mod, g, w_in, cos_t, sa_t, sb_t, w_pool, pool_scale, pool_gain)


def _attn_kernel(q_ref, k_ref, vt_ref, o_ref, km_ref, bias_ref, s0_ref, s1_ref, *, nb):
    blk = MOBA_BLOCK
    for j in range(nb):
        kj = k_ref[0, j * blk:(j + 1) * blk, :].astype(F32)
        km_ref[j:j + 1, :] = jnp.mean(kj, axis=0, keepdims=True)
    s_slots = (s0_ref, s1_ref)

    def tile(t, prev_accs):
        _attn_write(jnp.maximum(t - 1, 0), prev_accs, o_ref)
        return _attn_tile(t, q_ref, k_ref, vt_ref, km_ref, bias_ref, s_slots, nb)

    ntiles = nb // 2
    dummy = tuple(jnp.ones((VT_ROWS, PAIR_KEYS), F32) for _ in range(HEADS_PER_GROUP))
    _attn_write(ntiles - 1, lax.fori_loop(0, ntiles, tile, dummy), o_ref)


def _attn_write(t, accs, o_ref):
    rows = pl.ds(pl.multiple_of(t * PAIR_KEYS, PAIR_KEYS), PAIR_KEYS)
    outs = [acc[0:HEAD_DIM, :] / acc[HEAD_DIM:HEAD_DIM + 1, :] for acc in accs]
    o_ref[0, rows, :] = jnp.concatenate(outs, axis=0).T


def _attn_tile(t, q_ref, k_ref, vt_ref, km_ref, bias_ref, s_slots, nb):
    blk = MOBA_BLOCK
    tq = PAIR_KEYS
    heads = range(HEADS_PER_GROUP)
    rows = pl.ds(pl.multiple_of(t * tq, tq), tq)
    q2 = q_ref[0, rows, :]
    lane = lax.broadcasted_iota(jnp.int32, (1, LANES), 1)
    qh = [jnp.where((lane >= hh * HEAD_DIM) & (lane < (hh + 1) * HEAD_DIM), q2, jnp.zeros_like(q2))
          for hh in heads]

    def pair_keys(pi):
        return k_ref[0, pl.ds(pl.multiple_of(pi * PAIR_KEYS, PAIR_KEYS), PAIR_KEYS), :]

    def issue_scores(j, slot):
        kj = k_ref[0, pl.ds(pl.multiple_of(j * blk, blk), blk), :]
        maxes = []
        for hh in heads:
            s = lax.dot_general(kj, qh[hh], _NT, preferred_element_type=F32)
            s_slots[slot][hh] = s
            maxes.append(jnp.max(s, axis=0, keepdims=True))
        return tuple(maxes)

    def softmax_pv(hh, s, cmax, j, vt_blk, state):
        b = bias_ref[hh, j]
        m_new = jnp.maximum(state[0], cmax + b)
        p = jnp.exp2(s - (m_new - b)).astype(BF16)
        acc = jnp.exp2(state[0] - m_new) * state[1] + jnp.dot(vt_blk, p, preferred_element_type=F32)
        return m_new, acc

    km = km_ref[...]
    km_hi = km.astype(BF16)
    km_lo = (km - km_hi.astype(F32)).astype(BF16)
    kown = pair_keys(t)
    lhs = jnp.concatenate([km_hi, km_lo, kown[0:blk]], axis=0)
    r = [lax.dot_general(lhs, qh[hh], _NT, preferred_element_type=F32) for hh in heads]
    r2 = [lax.dot_general(kown[blk:], qh[hh][blk:], _NT, preferred_element_type=F32) for hh in heads]
    maxes0 = issue_scores(0, 0)

    jidx = lax.broadcasted_iota(jnp.int32, (nb, 1), 0)
    qblk = 2 * t + lax.broadcasted_iota(jnp.int32, (1, tq), 1) // blk
    past = jidx < qblk
    for hh in heads:
        gate = jnp.where(past, r[hh][0:nb] + r[hh][nb:2 * nb], NEG_INF)
        picked = jnp.zeros(gate.shape, jnp.bool_)
        for _ in range(min(MOBA_TOP_K, nb)):
            top = jnp.max(gate, axis=0, keepdims=True)
            first = jnp.min(jnp.where(gate == top, jidx, nb), axis=0, keepdims=True)
            hit = jidx == first
            picked = picked | hit
            gate = jnp.where(hit, -jnp.inf, gate)
        bias = jnp.where(picked & past, 0.0, NEG_INF).astype(F32)
        for j in range(nb):
            bias_ref[hh, j] = bias[j:j + 1, :]

    causal = (lax.broadcasted_iota(jnp.int32, (blk, blk), 0)
              <= lax.broadcasted_iota(jnp.int32, (blk, blk), 1))
    own = []
    for hh in heads:
        vt_own = vt_ref[0, hh, t]
        s_a = jnp.where(causal, r[hh][2 * nb:, 0:blk], NEG_INF)
        m_a = jnp.max(s_a, axis=0, keepdims=True)
        acc_a = jnp.dot(vt_own[:, 0:blk], jnp.exp2(s_a - m_a).astype(BF16), preferred_element_type=F32)
        s_p = r[hh][2 * nb:, blk:]
        s_o = jnp.where(causal, r2[hh], NEG_INF)
        b_p = bias_ref[hh, 2 * t][:, blk:]
        m_b = jnp.maximum(jnp.max(s_p, axis=0, keepdims=True) + b_p, jnp.max(s_o, axis=0, keepdims=True))
        p_b = jnp.concatenate([jnp.exp2(s_p - (m_b - b_p)).astype(BF16),
                               jnp.exp2(s_o - m_b).astype(BF16)], axis=0)
        acc_b = jnp.dot(vt_own, p_b, preferred_element_type=F32)
        own.append((jnp.concatenate([m_a, m_b], axis=1), jnp.concatenate([acc_a, acc_b], axis=1)))

    def trip(j, carry, slot, vt_blocks):
        state, maxes = carry
        nxt = issue_scores(jnp.minimum(j + 1, nb - 1), 1 - slot)
        return tuple(softmax_pv(hh, s_slots[slot][hh], maxes[hh], j, vt_blocks[hh], state[hh])
                     for hh in heads), nxt

    def pair_trips(u, carry):
        vt_pair = [vt_ref[0, hh, u] for hh in heads]
        carry = trip(2 * u, carry, 0, [v[:, 0:blk] for v in vt_pair])
        return trip(2 * u + 1, carry, 1, [v[:, blk:] for v in vt_pair])

    def two_pairs(v, carry):
        return pair_trips(2 * v + 1, pair_trips(2 * v, carry))

    def four_pairs(w, carry):
        return two_pairs(2 * w + 1, two_pairs(2 * w, carry))

    carry = lax.fori_loop(0, t // 4, four_pairs, (tuple(own), maxes0))
    carry = lax.cond(t % 4 >= 2, lambda: two_pairs(t // 4 * 2, carry), lambda: carry)
    carry = lax.cond(t % 2 == 1, lambda: pair_trips(t - 1, carry), lambda: carry)
    return tuple(acc for _, acc in carry[0])


def _attn_call(q, k, vt):
    b, s, aw = q.shape
    ngroups = aw // LANES
    nb = s // MOBA_BLOCK
    tq = PAIR_KEYS
    return pl.pallas_call(
        functools.partial(_attn_kernel, nb=nb),
        grid=(b, ngroups),
        in_specs=[pl.BlockSpec((1, s, LANES), lambda bi, gi: (bi, 0, gi)),
                  pl.BlockSpec((1, s, LANES), lambda bi, gi: (bi, 0, gi)),
                  pl.BlockSpec((1, HEADS_PER_GROUP, nb // 2, VT_ROWS, PAIR_KEYS),
                               lambda bi, gi: (bi, gi, 0, 0, 0))],
        out_specs=pl.BlockSpec((1, s, LANES), lambda bi, gi: (bi, 0, gi)),
        out_shape=jax.ShapeDtypeStruct((b, s, aw), F32),
        scratch_shapes=[pltpu.VMEM((nb, LANES), F32),
                        pltpu.VMEM((HEADS_PER_GROUP, nb, 1, tq), F32),
                        pltpu.VMEM((HEADS_PER_GROUP, MOBA_BLOCK, tq), F32),
                        pltpu.VMEM((HEADS_PER_GROUP, MOBA_BLOCK, tq), F32)],
        compiler_params=_params("arbitrary", "arbitrary"),
        name="moba_attn",
    )(q, k, vt)


def _ffn_kernel(ao_ref, pn_ref, xin_ref, ga_ref, wo_ref, gpm_ref, gt1_ref,
                sc_ref, sh_ref, g_ref, wu_ref, cw_ref, cb_ref, wd_ref, gp_ref, gt_ref,
                o_ref, tail_ref, act_ref, *, ts, tf):
    si = pl.program_id(1)
    aw = ao_ref.shape[2]
    parts = [slice(r * (ts // NORM_PARTS), (r + 1) * (ts // NORM_PARTS)) for r in range(NORM_PARTS)]
    xs, hbs = [], []
    for rows in parts:
        an = (_rms(ao_ref[0, rows, :]) * ga_ref[...]).astype(BF16)
        y = (jnp.dot(an, wo_ref[0:aw, :], preferred_element_type=F32)
             + jnp.dot(pn_ref[0, rows, :], wo_ref[aw:, :], preferred_element_type=F32))
        x = xin_ref[0, rows, :] + _rms(y) * (gt1_ref[0] * gpm_ref[...])
        xs.append(x)
        hbs.append((_rms(x) * (g_ref[...] * (1.0 + sc_ref[0])) + sh_ref[0]).astype(BF16))

    hb = jnp.concatenate(hbs, axis=0)
    d_ff = wd_ref.shape[0]
    first = si == 0

    def conv(lo):
        u = jnp.dot(hb, wu_ref[:, lo:lo + tf], preferred_element_type=F32)
        tail = jnp.where(first, 0.0, tail_ref[:, lo:lo + tf])
        tail_ref[:, lo:lo + tf] = u[ts - CONV_HALO:, :]
        ucat = jnp.concatenate([tail, u], axis=0)
        uc = cb_ref[:, lo:lo + tf] + cw_ref[CONV_WIDTH - 1:CONV_WIDTH, lo:lo + tf] * u
        for j in range(CONV_WIDTH - 1):
            back = CONV_WIDTH - 1 - j
            uc = uc + cw_ref[j:j + 1, lo:lo + tf] * ucat[CONV_HALO - back:CONV_HALO - back + ts, :]
        return uc

    for fi in range(d_ff // tf):
        a = conv(fi * tf)
        g = conv(d_ff + fi * tf)
        act_ref[:, fi * tf:(fi + 1) * tf] = (a * jax.nn.sigmoid(a) * g).astype(BF16)
    for rows, x in zip(parts, xs):
        y = jnp.dot(act_ref[rows, :], wd_ref[...], preferred_element_type=F32)
        o_ref[0, rows, :] = x + _rms(y) * (gt_ref[0] * gp_ref[...])


def _ffn_call(l, ao, pn, x, mod, attn_gain, w_out, g_post_mix,
              g_pre, w_up, conv_w, conv_b, w_down, g_post, ts=TOKEN_TILE, tf=256):
    b, s, d = x.shape
    f = w_down.shape[1]
    aw = ao.shape[2]
    pw = pn.shape[2]
    tok = lambda bi, si: (bi, si, 0)
    resident = dict(pipeline_mode=pl.Buffered(1))
    return pl.pallas_call(
        functools.partial(_ffn_kernel, ts=ts, tf=tf),
        grid=(b, s // ts),
        in_specs=[pl.BlockSpec((1, ts, aw), tok),
                  pl.BlockSpec((1, ts, pw), tok),
                  pl.BlockSpec((1, ts, d), tok),
                  _layer_spec(l, attn_gain),
                  _layer_spec(l, w_out, **resident),
                  _layer_spec(l, g_post_mix),
                  _mod_spec(l, MOD_GT1, d),
                  _mod_spec(l, MOD_SC2, d),
                  _mod_spec(l, MOD_SH2, d),
                  _layer_spec(l, g_pre),
                  _layer_spec(l, w_up, **resident),
                  _layer_spec(l, conv_w),
                  _layer_spec(l, conv_b),
                  _layer_spec(l, w_down, **resident),
                  _layer_spec(l, g_post),
                  _mod_spec(l, MOD_GT2, d)],
        out_specs=pl.BlockSpec((1, ts, d), tok),
        out_shape=jax.ShapeDtypeStruct((b, s, d), F32),
        scratch_shapes=[pltpu.VMEM((CONV_HALO, 2 * f), F32),
                        pltpu.VMEM((ts, f), BF16)],
        compiler_params=_params("arbitrary", "arbitrary"),
        name="conv_ffn",
    )(ao, pn, x, attn_gain, w_out, g_post_mix, mod,
      mod, mod, g_pre, w_up, conv_w, conv_b, w_down, g_post, mod)


def _rotary_tables(s):
    half = ROT_DIM // 2
    pos = jnp.arange(s, dtype=F32)
    inv_freq = jnp.power(jnp.float32(ROPE_THETA), -jnp.arange(0, ROT_DIM, 2, dtype=F32) / ROT_DIM)
    dim = np.arange(LANES) % HEAD_DIM
    freq = jnp.where(dim < ROT_DIM, inv_freq[dim % half], 0.0)
    ang = pos[:, None] * freq[None, :]
    cos = jnp.cos(ang)
    sin = jnp.sin(ang)
    sa = jnp.where(dim < half, -sin, 0.0)
    sb = jnp.where((dim >= half) & (dim < ROT_DIM), sin, 0.0)
    return cos, sa, sb


def kernel(x, c, w_ada, b_ada, g_pre_mix, w_in, w_pool, pool_scale, attn_out_gain, pool_out_gain,
           w_out, g_post_mix, g_pre_ffn, w_up, conv_w, conv_b, w_down, g_post_ffn):
    b, s, d = x.shape
    depth = w_ada.shape[0]
    assert s % TOKEN_TILE == 0 and d % LANES == 0
    mod = _mod_call(c, w_ada, b_ada).reshape(depth, b, 6, 1, d)
    cos_t, sa_t, sb_t = _rotary_tables(s)
    w_in, w_pool, w_out, w_up, w_down = (w.astype(BF16) for w in (w_in, w_pool, w_out, w_up, w_down))
    (g_pre_mix, pool_scale, pool_out_gain, attn_out_gain, g_post_mix, g_pre_ffn, conv_b,
     g_post_ffn) = (v[:, None, :] for v in (g_pre_mix, pool_scale, pool_out_gain, attn_out_gain,
                                            g_post_mix, g_pre_ffn, conv_b, g_post_ffn))
    for l in range(depth):
        q, k, vt, pn = _in_call(l, x, mod, g_pre_mix, w_in, cos_t, sa_t, sb_t, w_pool,
                                pool_scale, pool_out_gain)
        ao = _attn_call(q, k, vt)
        x = _ffn_call(l, ao, pn, x, mod, attn_out_gain, w_out, g_post_mix,
                      g_pre_ffn, w_up, conv_w, conv_b, w_down, g_post_ffn)
    return x
```
